```python
import math
import jax, jax.numpy as jnp
from jax import lax
import numpy as np

D_MODEL = 1024
BATCH = 8
SEQ = 8192
DEPTH = 1
DEC_BATCH = 32
DEC_SEQ = 2048
PAST_LEN = 128

GDN_HEADS = 4
GDN_DK = 128
GDN_DV = 128
CONV_WIDTH = 5
CONV_PAD = CONV_WIDTH // 2
CHUNK = 64
DIFF_HEADS = 4
DIFF_DQK = 64
DIFF_DV = 2 * DIFF_DQK
ROPE_THETA = 500000.0
ROPE_DIM = DIFF_DQK // 4
Q_BLOCK = 128
D_FF = int(math.ceil(8 * D_MODEL / 3 / 256)) * 256
ALPHA = (2 * DEPTH) ** 0.25
INIT_BETA = (8 * DEPTH) ** -0.25
GDN_CONV_CH = 2 * GDN_HEADS * GDN_DK + GDN_HEADS * GDN_DV
GDN_Z = GDN_HEADS * GDN_DV
GDN_GATES = 4 * GDN_HEADS
DIFF_QK = DIFF_HEADS * 2 * DIFF_DQK
DIFF_VW = DIFF_HEADS * DIFF_DV
IN_COLS = GDN_CONV_CH + GDN_Z + GDN_GATES + 2 * DIFF_QK + DIFF_VW
MIX_WIDTH = GDN_HEADS * GDN_DV + DIFF_HEADS * DIFF_DV

kernel_name = "hymba_gdn_diffattn_deepnorm_encoder"


def lambda_init_fn(layer):
    return 0.8 - 0.6 * math.exp(-0.3 * layer)


def layer_norm(x, g, b, eps=1e-5):
    xf = x.astype(jnp.float32)
    mu = jnp.mean(xf, -1, keepdims=True)
    var = jnp.mean(jnp.square(xf - mu), -1, keepdims=True)
    return ((xf - mu) * lax.rsqrt(var + eps) * g.astype(jnp.float32) + b.astype(jnp.float32)).astype(x.dtype)


def rms_norm(x, g, eps=1e-6):
    xf = x.astype(jnp.float32)
    return xf * lax.rsqrt(jnp.mean(xf * xf, -1, keepdims=True) + eps) * g.astype(jnp.float32)


def l2norm(t, eps=1e-6):
    return t * lax.rsqrt(jnp.sum(t * t, -1, keepdims=True) + eps)


def centred_depthwise_conv(x, w):
    return lax.conv_general_dilated(
        x, w[:, None, :].astype(x.dtype), window_strides=(1,), padding=[(CONV_PAD, CONV_PAD)],
        dimension_numbers=("NWC", "WIO", "NWC"), feature_group_count=x.shape[-1])


def delta_rule_chunked(q, k, v, g, beta):
    B, S, H, DK = q.shape
    DV = v.shape[-1]
    N = S // CHUNK
    ch = lambda t: jnp.moveaxis(t.reshape((B, N, CHUNK, H) + t.shape[3:]), 3, 1)
    q, k, v, g, beta = ch(q), ch(k), ch(v), ch(g), ch(beta)
    gc = jnp.cumsum(g, axis=-1)
    incl = jnp.tril(jnp.ones((CHUNK, CHUNK), bool))
    strict = jnp.tril(jnp.ones((CHUNK, CHUNK), bool), -1)
    decay = jnp.exp(jnp.where(incl, gc[..., :, None] - gc[..., None, :], -jnp.inf))
    kb = k * beta[..., None]
    m = jnp.where(strict, jnp.einsum('bhnid,bhnjd->bhnij', kb, k) * decay, 0.0)
    a_mat = m + jnp.eye(CHUNK, dtype=jnp.float32)
    rhs = jnp.concatenate([v * beta[..., None], kb * jnp.exp(gc)[..., None]], -1)
    sol = lax.linalg.triangular_solve(a_mat, rhs, left_side=True, lower=True, unit_diagonal=True)
    u, w = sol[..., :DV], sol[..., DV:]
    attn = jnp.einsum('bhnid,bhnjd->bhnij', q, k) * decay
    qd = q * jnp.exp(gc)[..., None]
    kd = k * jnp.exp(gc[..., -1:] - gc)[..., None]
    glast = jnp.exp(gc[..., -1])

    def step(state, xs):
        u_c, w_c, attn_c, qd_c, kd_c, gl_c = xs
        v_new = u_c - jnp.einsum('bhcd,bhde->bhce', w_c, state)
        o = jnp.einsum('bhcd,bhde->bhce', qd_c, state) + jnp.einsum('bhij,bhje->bhie', attn_c, v_new)
        state = state * gl_c[..., None, None] + jnp.einsum('bhcd,bhce->bhde', kd_c, v_new)
        return state, o

    xs = tuple(jnp.moveaxis(t, 2, 0) for t in (u, w, attn, qd, kd, glast))
    s0 = jnp.zeros((B, H, DK, DV), jnp.float32)
    _, o = lax.scan(step, s0, xs)
    return jnp.transpose(o, (1, 0, 3, 2, 4)).reshape(B, S, H, DV)


def gated_deltanet_group(qkv, z, gates, conv_w, a_log, dt_bias, norm_g):
    B, S, _ = qkv.shape
    c = jax.nn.silu(centred_depthwise_conv(qkv, conv_w)).astype(jnp.float32)
    q, k, v = jnp.split(c, [GDN_HEADS * GDN_DK, 2 * GDN_HEADS * GDN_DK], axis=-1)
    q = l2norm(q.reshape(B, S, GDN_HEADS, GDN_DK)) * (GDN_DK ** -0.5)
    k = l2norm(k.reshape(B, S, GDN_HEADS, GDN_DK))
    v = v.reshape(B, S, GDN_HEADS, GDN_DV)
    gt = gates.astype(jnp.float32).reshape(B, S, 4, GDN_HEADS)
    beta = jax.nn.sigmoid(gt[:, :, 0:2])
    g = -jnp.exp(a_log.astype(jnp.float32)) * jax.nn.softplus(gt[:, :, 2:4] + dt_bias.astype(jnp.float32))
    o_f = delta_rule_chunked(q, k, v, g[:, :, 0], beta[:, :, 0])
    flip = lambda t: jnp.flip(t, axis=1)
    o_b = flip(delta_rule_chunked(flip(q), flip(k), flip(v), flip(g[:, :, 1]), flip(beta[:, :, 1])))
    o = rms_norm(o_f + o_b, norm_g) * jax.nn.silu(z.astype(jnp.float32).reshape(B, S, GDN_HEADS, GDN_DV))
    return o.reshape(B, S, GDN_HEADS * GDN_DV)


def partial_rope(x, cos, sin):
    half = ROPE_DIM // 2
    x = x.astype(jnp.float32)
    c = cos[:, None, None, :]
    s = sin[:, None, None, :]
    x1, x2, rest = x[..., :half], x[..., half:ROPE_DIM], x[..., ROPE_DIM:]
    return jnp.concatenate([x1 * c - x2 * s, x2 * c + x1 * s, rest], -1)


def diff_attention_group(dq, dk, dv, lam_qk, norm_g, lam_init):
    B, S, _ = dq.shape
    inv = 1.0 / (ROPE_THETA ** (jnp.arange(0, ROPE_DIM, 2, dtype=jnp.float32) / ROPE_DIM))
    ang = jnp.arange(S, dtype=jnp.float32)[:, None] * inv[None, :]
    cos, sin = jnp.cos(ang), jnp.sin(ang)
    q = partial_rope(dq.reshape(B, S, DIFF_HEADS, 2, DIFF_DQK), cos, sin) * (DIFF_DQK ** -0.5)
    k = partial_rope(dk.reshape(B, S, DIFF_HEADS, 2, DIFF_DQK), cos, sin)
    v = dv.astype(jnp.float32).reshape(B, S, DIFF_HEADS, DIFF_DV)
    lq = lam_qk.astype(jnp.float32)
    lam = jnp.exp(jnp.sum(lq[0] * lq[1])) - jnp.exp(jnp.sum(lq[2] * lq[3])) + lam_init
    nb = S // Q_BLOCK
    qb = jnp.moveaxis(q.reshape(B, nb, Q_BLOCK, DIFF_HEADS, 2, DIFF_DQK), 1, 0)

    def block(qi):
        p = jax.nn.softmax(jnp.einsum('bqhcd,bkhcd->bhcqk', qi, k), axis=-1)
        wts = p[:, :, 0] - lam * p[:, :, 1]
        return jnp.einsum('bhqk,bkhd->bqhd', wts, v)

    o = jnp.moveaxis(lax.map(block, qb), 0, 1).reshape(B, S, DIFF_HEADS, DIFF_DV)
    o = rms_norm(o, norm_g) * (1.0 - lam_init)
    return o.reshape(B, S, DIFF_HEADS * DIFF_DV)


def hybrid_mixer(x, w_in, conv_w, a_log, dt_bias, gdn_norm_g, lam_qk, diff_norm_g, w_out, lam_init):
    h = x @ w_in
    splits = np.cumsum([GDN_CONV_CH, GDN_Z, GDN_GATES, DIFF_QK, DIFF_QK]).tolist()
    qkv, z, gates, dq, dk, dv = jnp.split(h, splits, axis=-1)
    o_a = gated_deltanet_group(qkv, z, gates, conv_w, a_log, dt_bias, gdn_norm_g)
    o_b = diff_attention_group(dq, dk, dv, lam_qk, diff_norm_g, lam_init)
    o = jnp.concatenate([o_a, o_b], -1).astype(x.dtype)
    return o @ w_out


def swiglu(x, w_gate_up, w_down):
    gate, up = jnp.split(x @ w_gate_up, 2, axis=-1)
    return (jax.nn.silu(gate) * up) @ w_down


def setup_inputs(seed: int = 0) -> dict:
    key = jax.random.key(seed)
    ks = jax.random.split(key, 20)
    f32 = jnp.float32
    nrm = lambda k, shape, s: jax.random.normal(k, shape, f32) * s
    dt = jnp.exp(jax.random.uniform(ks[5], (DEPTH, 2, GDN_HEADS), f32, math.log(1e-3), math.log(1e-1)))
    return {
        "x_prompt": jax.random.normal(ks[0], (BATCH, SEQ, D_MODEL), f32),
        "x_sample": jax.random.normal(ks[1], (DEC_BATCH, DEC_SEQ, D_MODEL), f32),
        "w_in": nrm(ks[2], (DEPTH, D_MODEL, IN_COLS), D_MODEL ** -0.5),
        "conv_w": nrm(ks[3], (DEPTH, CONV_WIDTH, GDN_CONV_CH), CONV_WIDTH ** -0.5),
        "a_log": jnp.log(jax.random.uniform(ks[4], (DEPTH, 2, GDN_HEADS), f32, 1.0, 16.0)),
        "dt_bias": dt + jnp.log(-jnp.expm1(-dt)),
        "gdn_norm_g": 1.0 + nrm(ks[6], (DEPTH, GDN_DV), 0.02),
        "lam_qk": nrm(ks[7], (DEPTH, 4, DIFF_DQK), 0.1),
        "diff_norm_g": 1.0 + nrm(ks[8], (DEPTH, DIFF_DV), 0.02),
        "w_out": nrm(ks[9], (DEPTH, MIX_WIDTH, D_MODEL), MIX_WIDTH ** -0.5 * INIT_BETA),
        "ln1_g": 1.0 + nrm(ks[10], (DEPTH, D_MODEL), 0.02),
        "ln1_b": nrm(ks[11], (DEPTH, D_MODEL), 0.02),
        "w_gate_up": nrm(ks[12], (DEPTH, D_MODEL, 2 * D_FF), D_MODEL ** -0.5),
        "w_down": nrm(ks[13], (DEPTH, D_FF, D_MODEL), D_FF ** -0.5 * INIT_BETA),
        "ln2_g": 1.0 + nrm(ks[14], (DEPTH, D_MODEL), 0.02),
        "ln2_b": nrm(ks[15], (DEPTH, D_MODEL), 0.02),
    }


def reference(x_prompt, x_sample, w_in, conv_w, a_log, dt_bias, gdn_norm_g, lam_qk, diff_norm_g,
              w_out, ln1_g, ln1_b, w_gate_up, w_down, ln2_g, ln2_b):
    def trunk(x):
        for l in range(DEPTH):
            mix = hybrid_mixer(x, w_in[l], conv_w[l], a_log[l], dt_bias[l], gdn_norm_g[l], lam_qk[l],
                               diff_norm_g[l], w_out[l], lambda_init_fn(l))
            x = layer_norm(ALPHA * x + mix, ln1_g[l], ln1_b[l])
            x = layer_norm(ALPHA * x + swiglu(x, w_gate_up[l], w_down[l]), ln2_g[l], ln2_b[l])
        return x

    y_prompt = trunk(x_prompt)
    y_sample = trunk(x_sample)
    return (y_prompt, y_sample)
```

```python
import functools
import math

import jax
import jax.numpy as jnp
from jax import lax
from jax.experimental import pallas as pl
from jax.experimental.pallas import tpu as pltpu

F32 = jnp.float32
BF16 = jnp.bfloat16

D_MODEL = 1024
HEADS = 4
HEAD_DIM = 128
GDN_QKV = 3 * HEADS * HEAD_DIM
GDN_Z = HEADS * HEAD_DIM
GDN_GATES = 4 * HEADS
DIFF_DQK = 64
DIFF_W = HEADS * 2 * DIFF_DQK
CONV_WIDTH = 5
CONV_PAD = CONV_WIDTH // 2
CHUNK = 64
ROPE_THETA = 500000.0
ROPE_DIM = DIFF_DQK // 4
ROPE_HALF = ROPE_DIM // 2
D_FF = int(math.ceil(8 * D_MODEL / 3 / 256)) * 256

LANES = 128
SUBLANES = 8
VMEM_LIMIT = 56 * 1024 * 1024

ROW_TILE = 512
GDN_LOCAL_ROWS = 256
GDN_SCAN_CHUNKS = 8
ATTN_Q_BLOCK = 256
ATTN_K_BLOCK = 256


def _params(*sem):
    return pltpu.CompilerParams(dimension_semantics=sem, vmem_limit_bytes=VMEM_LIMIT)


def _sigmoid(x):
    return 1.0 / (1.0 + jnp.exp(-x))


def _softplus(x):
    return jnp.maximum(x, 0.0) + jnp.log(1.0 + jnp.exp(-jnp.abs(x)))


def _dot(a, b):
    return jnp.dot(a, b, preferred_element_type=F32)


def _dot_nt(a, b):
    return lax.dot_general(a, b, (((1,), (1,)), ((), ())), preferred_element_type=F32)


def _dot_tn(a, b):
    return lax.dot_general(a, b, (((0,), (0,)), ((), ())), preferred_element_type=F32)


def _layer_norm(y, g, b):
    mu = jnp.mean(y, -1, keepdims=True)
    d = y - mu
    var = jnp.mean(d * d, -1, keepdims=True)
    return d * lax.rsqrt(var + 1e-5) * g + b


def _inproj_kernel(x_ref, wn_ref, wt_ref, cos_ref, sin_ref, kc_ref, ks1_ref, ks2_ref,
                   qkv_ref, z_ref, g_ref, gt_ref, qt_ref, k_ref, vt_ref):
    xb = x_ref[...].astype(BF16)
    hn = _dot(xb, wn_ref[...])
    qkv_ref[...] = hn[:, :GDN_QKV]
    z_ref[...] = hn[:, GDN_QKV:GDN_QKV + GDN_Z]
    g_ref[...] = hn[:, GDN_QKV + GDN_Z:GDN_QKV + GDN_Z + LANES]
    dk0 = GDN_QKV + GDN_Z + LANES
    kc, ks1, ks2 = kc_ref[...], ks1_ref[...], ks2_ref[...]
    for h in range(HEADS):
        kh = hn[:, dk0 + h * LANES:dk0 + (h + 1) * LANES]
        up = pltpu.roll(kh, LANES - ROPE_HALF, axis=1)
        dn = pltpu.roll(kh, ROPE_HALF, axis=1)
        k_ref[:, h * LANES:(h + 1) * LANES] = (kh * kc + up * ks1 + dn * ks2).astype(BF16)
    ht = _dot_nt(wt_ref[...], xb)
    vt_ref[...] = ht[DIFF_W:2 * DIFF_W].astype(BF16)
    gt_ref[...] = ht[2 * DIFF_W:2 * DIFF_W + GDN_GATES]
    cos, sin = cos_ref[...], sin_ref[...]
    pieces = []
    for c in range(DIFF_W // DIFF_DQK):
        r = c * DIFF_DQK
        a, b = ht[r:r + ROPE_HALF], ht[r + ROPE_HALF:r + ROPE_DIM]
        pieces += [a * cos - b * sin, b * cos + a * sin, ht[r + ROPE_DIM:r + DIFF_DQK]]
    qt_ref[...] = (jnp.concatenate(pieces, axis=0) * (DIFF_DQK ** -0.5)).astype(BF16)


def _inproj(x, wn, wt, rope, seq):
    t = x.shape[0]
    tm = ROW_TILE
    nps = seq // tm
    cos_t, sin_t, kc, ks1, ks2 = rope
    row = lambda i: (i, 0)
    col = lambda i: (0, i)
    const = lambda i: (0, 0)
    pos_row = lambda i: (i % nps, 0)
    pos_col = lambda i: (0, i % nps)
    n_cols = wn.shape[1]
    t_rows = wt.shape[0]
    return pl.pallas_call(
        _inproj_kernel,
        grid=(t // tm,),
        in_specs=[
            pl.BlockSpec((tm, D_MODEL), row),
            pl.BlockSpec((D_MODEL, n_cols), const),
            pl.BlockSpec((t_rows, D_MODEL), const),
            pl.BlockSpec((ROPE_HALF, tm), pos_col),
            pl.BlockSpec((ROPE_HALF, tm), pos_col),
            pl.BlockSpec((tm, LANES), pos_row),
            pl.BlockSpec((tm, LANES), pos_row),
            pl.BlockSpec((tm, LANES), pos_row),
        ],
        out_specs=[
            pl.BlockSpec((tm, GDN_QKV), row),
            pl.BlockSpec((tm, GDN_Z), row),
            pl.BlockSpec((tm, LANES), row),
            pl.BlockSpec((GDN_GATES, tm), col),
            pl.BlockSpec((DIFF_W, tm), col),
            pl.BlockSpec((tm, DIFF_W), row),
            pl.BlockSpec((DIFF_W, tm), col),
        ],
        out_shape=[
            jax.ShapeDtypeStruct((t, GDN_QKV), F32),
            jax.ShapeDtypeStruct((t, GDN_Z), F32),
            jax.ShapeDtypeStruct((t, LANES), F32),
            jax.ShapeDtypeStruct((GDN_GATES, t), F32),
            jax.ShapeDtypeStruct((DIFF_W, t), BF16),
            jax.ShapeDtypeStruct((t, DIFF_W), BF16),
            jax.ShapeDtypeStruct((DIFF_W, t), BF16),
        ],
        compiler_params=_params("parallel"),
        name="inproj",
    )(x, wn, wt, cos_t, sin_t, kc, ks1, ks2)


def _unit_tri_inverse(m):
    n = m.shape[0]
    ri = lax.broadcasted_iota(jnp.int32, (n, n), 0)
    ci = lax.broadcasted_iota(jnp.int32, (n, n), 1)
    same16 = (ri // 16) == (ci // 16)
    same32 = (ri // 32) == (ci // 32)
    mm = lambda a, b: _dot(a.astype(BF16), b.astype(BF16))
    d = jnp.where(same16, m, 0.0)
    c32 = jnp.where(jnp.logical_and(same32, jnp.logical_not(same16)), m, 0.0)
    c64 = jnp.where(same32, 0.0, m)
    d2 = mm(d, d)
    d4 = mm(d2, d2)
    d8 = mm(d4, d4)
    t = jnp.where(ri == ci, 1.0, 0.0) - d
    t = t + mm(t, d2)
    t = t + mm(t, d4)
    t = t + mm(t, d8)
    t = t - mm(t, mm(c32, t))
    t = t - mm(t, mm(c64, t))
    return t


def _gdn_local_kernel(qkv_ref, prev_ref, next_ref, g_ref, gt_ref, cw_ref, prow_ref, pcol_ref,
                      uf_ref, wf_ref, qdf_ref, kdf_ref, af_ref,
                      ub_ref, wb_ref, qdb_ref, kdb_ref, ab_ref, gl_ref, *, blocks_per_seq):
    sb = qkv_ref.shape[0]
    j = pl.program_id(0) % blocks_per_seq
    prev_on = j > 0
    next_on = j < blocks_per_seq - 1
    ext_rows = sb + 2 * SUBLANES

    conv = []
    for cg in range(GDN_QKV // LANES):
        sl = slice(cg * LANES, (cg + 1) * LANES)
        ext = jnp.concatenate([jnp.where(prev_on, prev_ref[:, sl], 0.0), qkv_ref[:, sl],
                               jnp.where(next_on, next_ref[:, sl], 0.0)], axis=0)
        acc = None
        for tap in range(CONV_WIDTH):
            shift = (CONV_PAD - tap) % ext_rows
            rolled = ext if shift == 0 else pltpu.roll(ext, shift, axis=0)
            term = rolled[SUBLANES:SUBLANES + sb] * cw_ref[tap:tap + 1, sl]
            acc = term if acc is None else acc + term
        conv.append(acc * _sigmoid(acc))

    def l2n(t):
        return t * lax.rsqrt(jnp.sum(t * t, -1, keepdims=True) + 1e-6)

    qn = [l2n(conv[h]) * (HEAD_DIM ** -0.5) for h in range(HEADS)]
    kn = [l2n(conv[HEADS + h]) for h in range(HEADS)]
    vv = [conv[2 * HEADS + h] for h in range(HEADS)]

    g_c = g_ref[...]
    beta_c = _sigmoid(g_c)
    gval_c = -jnp.exp(prow_ref[0:1, :]) * _softplus(g_c + prow_ref[1:2, :])
    g_r = gt_ref[...]
    gval_r = -jnp.exp(pcol_ref[:, 0:1]) * _softplus(g_r + pcol_ref[:, 1:2])

    bi = lax.broadcasted_iota(jnp.int32, (sb, sb), 0)
    bj = lax.broadcasted_iota(jnp.int32, (sb, sb), 1)
    same_chunk = (bi // CHUNK) == (bj // CHUNK)
    low = jnp.where(jnp.logical_and(same_chunk, bj <= bi), 1.0, 0.0).astype(F32)
    upp = jnp.where(jnp.logical_and(same_chunk, bj >= bi), 1.0, 0.0).astype(F32)
    hdot = functools.partial(jnp.dot, preferred_element_type=F32, precision=lax.Precision.HIGHEST)
    gc_c = (hdot(low, gval_c), hdot(upp, gval_c))
    gc_r = (hdot(gval_r, upp), hdot(gval_r, low))

    ri = lax.broadcasted_iota(jnp.int32, (CHUNK, CHUNK), 0)
    ci = lax.broadcasted_iota(jnp.int32, (CHUNK, CHUNK), 1)
    incl = (ci <= ri, ci >= ri)
    strict = (ci < ri, ci > ri)
    row8 = lax.broadcasted_iota(jnp.int32, (SUBLANES, LANES), 0)
    u_refs, w_refs, qd_refs, kd_refs, a_refs = (uf_ref, ub_ref), (wf_ref, wb_ref), (qdf_ref, qdb_ref), (kdf_ref, kdb_ref), (af_ref, ab_ref)

    for c in range(sb // CHUNK):
        r0 = c * CHUNK
        rows = slice(r0, r0 + CHUNK)
        gl_f = jnp.exp(gc_r[0][SUBLANES:2 * SUBLANES, r0 + CHUNK - 1:r0 + CHUNK])
        gl_b = jnp.exp(gc_r[1][SUBLANES:2 * SUBLANES, r0:r0 + 1])
        gl_ref[c * SUBLANES:(c + 1) * SUBLANES, :] = jnp.where(row8 < HEADS, gl_f, gl_b)
        for h in range(HEADS):
            hs = slice(h * HEAD_DIM, (h + 1) * HEAD_DIM)
            qh, kh, vh = qn[h][rows], kn[h][rows], vv[h][rows]
            qb, kb = qh.astype(BF16), kh.astype(BF16)
            kk = _dot_nt(kb, kb)
            qk = _dot_nt(qb, kb)
            for d in range(2):
                lane = 2 * HEADS + HEADS * d + h
                gcol = gc_c[d][rows, lane:lane + 1]
                grow = gc_r[d][lane:lane + 1, rows]
                bcol = beta_c[rows, HEADS * d + h:HEADS * d + h + 1]
                decay = jnp.exp(jnp.where(incl[d], gcol - grow, -jnp.inf))
                m = jnp.where(strict[d], kk * bcol * decay, 0.0)
                tinv = _unit_tri_inverse(m)
                eg = jnp.exp(gcol)
                rhs = jnp.concatenate([vh * bcol, kh * (bcol * eg)], axis=1).astype(BF16)
                sol = _dot(tinv.astype(BF16), rhs)
                glast = gcol[CHUNK - 1:CHUNK] if d == 0 else gcol[0:1]
                u_refs[d][rows, hs] = sol[:, :HEAD_DIM]
                w_refs[d][rows, hs] = sol[:, HEAD_DIM:].astype(BF16)
                qd_refs[d][rows, hs] = (qh * eg).astype(BF16)
                kd_refs[d][rows, hs] = (kh * jnp.exp(glast - gcol)).astype(BF16)
                a_refs[d][h, rows, :] = (qk * decay).astype(BF16)


def _gdn_local(qkv, gates, gates_t, conv_w8, prow, pcol, seq):
    t = qkv.shape[0]
    sb = GDN_LOCAL_ROWS
    bps = seq // sb
    hb = sb // SUBLANES
    n8 = t // SUBLANES
    row = lambda i: (i, 0)
    const = lambda i: (0, 0)
    wide = jax.ShapeDtypeStruct((t, HEADS * HEAD_DIM), F32)
    wide_bf = jax.ShapeDtypeStruct((t, HEADS * HEAD_DIM), BF16)
    attn = jax.ShapeDtypeStruct((HEADS, t, CHUNK), BF16)
    wide_spec = pl.BlockSpec((sb, HEADS * HEAD_DIM), row)
    attn_spec = pl.BlockSpec((HEADS, sb, CHUNK), lambda i: (0, i, 0))
    dir_shapes = [wide, wide_bf, wide_bf, wide_bf, attn]
    dir_specs = [wide_spec, wide_spec, wide_spec, wide_spec, attn_spec]
    return pl.pallas_call(
        functools.partial(_gdn_local_kernel, blocks_per_seq=bps),
        grid=(t // sb,),
        in_specs=[
            pl.BlockSpec((sb, GDN_QKV), row),
            pl.BlockSpec((SUBLANES, GDN_QKV), lambda i: (jnp.maximum(i * hb - 1, 0), 0)),
            pl.BlockSpec((SUBLANES, GDN_QKV), lambda i: (jnp.minimum((i + 1) * hb, n8 - 1), 0)),
            pl.BlockSpec((sb, LANES), row),
            pl.BlockSpec((GDN_GATES, sb), lambda i: (0, i)),
            pl.BlockSpec((SUBLANES, GDN_QKV), const),
            pl.BlockSpec((SUBLANES, LANES), const),
            pl.BlockSpec((GDN_GATES, LANES), const),
        ],
        out_specs=dir_specs + dir_specs + [pl.BlockSpec((sb // CHUNK * SUBLANES, LANES), row)],
        out_shape=dir_shapes + dir_shapes + [jax.ShapeDtypeStruct((t // CHUNK * SUBLANES, LANES), F32)],
        compiler_params=_params("parallel"),
        name="gdn_local",
    )(qkv, qkv, qkv, gates, gates_t, conv_w8, prow, pcol)


def _gdn_scan_kernel(uf_ref, wf_ref, qdf_ref, kdf_ref, af_ref, glf_ref,
                     ub_ref, wb_ref, qdb_ref, kdb_ref, ab_ref, glb_ref,
                     of_ref, ob_ref, state_ref, *, chunks):
    @pl.when(pl.program_id(1) == 0)
    def _():
        state_ref[...] = jnp.zeros_like(state_ref)

    dirs = ((uf_ref, wf_ref, qdf_ref, kdf_ref, af_ref, glf_ref, of_ref),
            (ub_ref, wb_ref, qdb_ref, kdb_ref, ab_ref, glb_ref, ob_ref))

    def body(c, carry):
        for d in range(2):
            u_ref, w_ref, qd_ref, kd_ref, a_ref, gl_ref, o_ref = dirs[d]
            ch = c if d == 0 else chunks - 1 - c
            rows = pl.ds(pl.multiple_of(ch * CHUNK, CHUNK), CHUNK)
            for h in range(HEADS):
                hs = slice(h * HEAD_DIM, (h + 1) * HEAD_DIM)
                s = state_ref[d * HEADS + h]
                sb16 = s.astype(BF16)
                v_new = u_ref[rows, hs] - _dot(w_ref[rows, hs], sb16)
                vb = v_new.astype(BF16)
                o_ref[rows, hs] = _dot(qd_ref[rows, hs], sb16) + _dot(a_ref[h, rows, :], vb)
                gl = gl_ref[pl.ds(ch * SUBLANES + d * HEADS + h, 1), :]
                state_ref[d * HEADS + h] = s * gl + _dot_tn(kd_ref[rows, hs], vb)
        return carry

    lax.fori_loop(0, chunks, body, 0)


def _gdn_scan(local_out, batch, seq):
    uf, wf, qdf, kdf, af, ub, wb, qdb, kdb, ab, gl = local_out
    t = uf.shape[0]
    cb = GDN_SCAN_CHUNKS
    rb = cb * CHUNK
    nb = seq // rb
    fwd = lambda b, j: (b * nb + j, 0)
    bwd = lambda b, j: (b * nb + nb - 1 - j, 0)
    fwd3 = lambda b, j: (0, b * nb + j, 0)
    bwd3 = lambda b, j: (0, b * nb + nb - 1 - j, 0)

    def dir_specs(m2, m3):
        wide = pl.BlockSpec((rb, HEADS * HEAD_DIM), m2)
        return [wide, wide, wide, wide, pl.BlockSpec((HEADS, rb, CHUNK), m3),
                pl.BlockSpec((cb * SUBLANES, LANES), m2)]

    out = jax.ShapeDtypeStruct((t, HEADS * HEAD_DIM), F32)
    return pl.pallas_call(
        functools.partial(_gdn_scan_kernel, chunks=cb),
        grid=(batch, nb),
        in_specs=dir_specs(fwd, fwd3) + dir_specs(bwd, bwd3),
        out_specs=[pl.BlockSpec((rb, HEADS * HEAD_DIM), fwd), pl.BlockSpec((rb, HEADS * HEAD_DIM), bwd)],
        out_shape=[out, out],
        scratch_shapes=[pltpu.VMEM((2 * HEADS, HEAD_DIM, HEAD_DIM), F32)],
        compiler_params=_params("parallel", "arbitrary"),
        name="gdn_scan",
    )(uf, wf, qdf, kdf, af, gl, ub, wb, qdb, kdb, ab, gl)


def _attn_kernel(qt_ref, k_ref, vt_ref, lq_ref, g_ref, o_ref, *, seq, kblk, lam_init):
    qb = qt_ref.shape[1]
    qt = qt_ref[...]
    row = lax.broadcasted_iota(jnp.int32, (2 * DIFF_DQK, 2 * qb), 0)
    colq = lax.broadcasted_iota(jnp.int32, (2 * DIFF_DQK, 2 * qb), 1)
    keep = (row < DIFF_DQK) == (colq < qb)
    rhs = jnp.where(keep, jnp.concatenate([qt, qt], axis=1), jnp.zeros((), BF16))

    def body(i, carry):
        m, l, acc = carry
        off = pl.multiple_of(i * kblk, kblk)
        s = _dot(k_ref[pl.ds(off, kblk), :], rhs)
        m_new = jnp.maximum(m, jnp.max(s, axis=0, keepdims=True))
        alpha = jnp.exp(m - m_new)
        p = jnp.exp(s - m_new)
        l = alpha * l + jnp.sum(p, axis=0, keepdims=True)
        acc = alpha * acc + _dot(vt_ref[:, pl.ds(off, kblk)], p.astype(BF16))
        return m_new, l, acc

    m0 = jnp.full((1, 2 * qb), -jnp.inf, F32)
    l0 = jnp.zeros((1, 2 * qb), F32)
    acc0 = jnp.zeros((HEAD_DIM, 2 * qb), F32)
    _, l, acc = lax.fori_loop(0, seq // kblk, body, (m0, l0, acc0))

    lq = lq_ref[...]
    lam = (jnp.exp(jnp.sum(lq[0:1] * lq[1:2], axis=-1, keepdims=True))
           - jnp.exp(jnp.sum(lq[2:3] * lq[3:4], axis=-1, keepdims=True)) + lam_init)
    on = acc / l
    o = (on[:, :qb] - lam * on[:, qb:]).T
    o = o * lax.rsqrt(jnp.mean(o * o, -1, keepdims=True) + 1e-6) * g_ref[...]
    o_ref[...] = o * (1.0 - lam_init)


def _attention(qt, k, vt, lam_qk, norm_g, batch, seq, lam_init):
    t = k.shape[0]
    qb = ATTN_Q_BLOCK
    nq = seq // qb
    return pl.pallas_call(
        functools.partial(_attn_kernel, seq=seq, kblk=ATTN_K_BLOCK, lam_init=lam_init),
        grid=(batch, HEADS, nq),
        in_specs=[
            pl.BlockSpec((2 * DIFF_DQK, qb), lambda b, h, i: (h, b * nq + i)),
            pl.BlockSpec((seq, 2 * DIFF_DQK), lambda b, h, i: (b, h)),
            pl.BlockSpec((HEAD_DIM, seq), lambda b, h, i: (h, b)),
            pl.BlockSpec((4, DIFF_DQK), lambda b, h, i: (0, 0)),
            pl.BlockSpec((1, HEAD_DIM), lambda b, h, i: (0, 0)),
        ],
        out_specs=pl.BlockSpec((qb, HEAD_DIM), lambda b, h, i: (b * nq + i, h)),
        out_shape=jax.ShapeDtypeStruct((t, HEADS * HEAD_DIM), F32),
        compiler_params=_params("parallel", "parallel", "arbitrary"),
        name="diff_attn",
    )(qt, k, vt, lam_qk, norm_g)


def _outproj_kernel(of_ref, ob_ref, z_ref, oa_ref, x_ref, w_ref, gg_ref, lg_ref, lb_ref, y_ref, *, alpha):
    parts = []
    for h in range(HEADS):
        hs = slice(h * HEAD_DIM, (h + 1) * HEAD_DIM)
        o = of_ref[:, hs] + ob_ref[:, hs]
        o = o * lax.rsqrt(jnp.mean(o * o, -1, keepdims=True) + 1e-6) * gg_ref[...]
        zz = z_ref[:, hs]
        parts.append(o * (zz * _sigmoid(zz)))
    mixed = jnp.concatenate(parts + [oa_ref[...]], axis=1).astype(BF16)
    y = alpha * x_ref[...] + _dot(mixed, w_ref[...])
    y_ref[...] = _layer_norm(y, lg_ref[...], lb_ref[...])


def _outproj(o_f, o_b, z, o_attn, x, w_out, gdn_g, ln_g, ln_b, alpha):
    t = x.shape[0]
    tm = ROW_TILE
    row = lambda i: (i, 0)
    const = lambda i: (0, 0)
    half = pl.BlockSpec((tm, HEADS * HEAD_DIM), row)
    return pl.pallas_call(
        functools.partial(_outproj_kernel, alpha=alpha),
        grid=(t // tm,),
        in_specs=[half, half, half, half,
                  pl.BlockSpec((tm, D_MODEL), row),
                  pl.BlockSpec((D_MODEL, D_MODEL), const),
                  pl.BlockSpec((1, HEAD_DIM), const),
                  pl.BlockSpec((1, D_MODEL), const),
                  pl.BlockSpec((1, D_MODEL), const)],
        out_specs=pl.BlockSpec((tm, D_MODEL), row),
        out_shape=jax.ShapeDtypeStruct((t, D_MODEL), F32),
        compiler_params=_params("parallel"),
        name="outproj_ln",
    )(o_f, o_b, z, o_attn, x, w_out, gdn_g, ln_g, ln_b)


def _ffn_kernel(x_ref, wg_ref, wu_ref, wd_ref, lg_ref, lb_ref, y_ref, *, alpha):
    x = x_ref[...]
    xb = x.astype(BF16)
    gate = _dot(xb, wg_ref[...])
    up = _dot(xb, wu_ref[...])
    act = (gate * _sigmoid(gate) * up).astype(BF16)
    y = alpha * x + _dot(act, wd_ref[...])
    y_ref[...] = _layer_norm(y, lg_ref[...], lb_ref[...])


def _ffn(x, wg, wu, wd, ln_g, ln_b, alpha):
    t = x.shape[0]
    tm = ROW_TILE // 2
    row = lambda i: (i, 0)
    const = lambda i: (0, 0)
    once = pl.Buffered(1)
    return pl.pallas_call(
        functools.partial(_ffn_kernel, alpha=alpha),
        grid=(t // tm,),
        in_specs=[pl.BlockSpec((tm, D_MODEL), row),
                  pl.BlockSpec((D_MODEL, D_FF), const, pipeline_mode=once),
                  pl.BlockSpec((D_MODEL, D_FF), const, pipeline_mode=once),
                  pl.BlockSpec((D_FF, D_MODEL), const, pipeline_mode=once),
                  pl.BlockSpec((1, D_MODEL), const),
                  pl.BlockSpec((1, D_MODEL), const)],
        out_specs=pl.BlockSpec((tm, D_MODEL), row),
        out_shape=jax.ShapeDtypeStruct((t, D_MODEL), F32),
        compiler_params=_params("parallel"),
        name="swiglu_ln",
    )(x, wg, wu, wd, ln_g, ln_b)


def _rope_tables(seq):
    inv = 1.0 / (ROPE_THETA ** (jnp.arange(0, ROPE_DIM, 2, dtype=F32) / ROPE_DIM))
    ang = jnp.arange(seq, dtype=F32)[:, None] * inv[None, :]
    cos, sin = jnp.cos(ang), jnp.sin(ang)
    zero8 = jnp.zeros_like(sin)
    rest = DIFF_DQK - ROPE_DIM
    c64 = jnp.concatenate([cos, cos, jnp.ones((seq, rest), F32)], axis=1)
    s1 = jnp.concatenate([-sin, zero8, jnp.zeros((seq, rest), F32)], axis=1)
    s2 = jnp.concatenate([zero8, sin, jnp.zeros((seq, rest), F32)], axis=1)
    two = lambda a: jnp.concatenate([a, a], axis=1)
    return cos.T, sin.T, two(c64), two(s1), two(s2)


def _prep_layer(w_in, conv_w, a_log, dt_bias, gdn_norm_g, lam_qk, diff_norm_g, w_out, ln1_g, ln1_b,
                w_gate_up, w_down, ln2_g, ln2_b):
    c0 = GDN_QKV
    c1 = c0 + GDN_Z
    c2 = c1 + GDN_GATES
    c3 = c2 + DIFF_W
    c4 = c3 + DIFF_W
    w_g = w_in[:, c1:c2]
    w_g_pad = jnp.pad(w_g, ((0, 0), (0, LANES - GDN_GATES)))
    wn = jnp.concatenate([w_in[:, :c1], w_g_pad, w_in[:, c3:c4]], axis=1).astype(BF16)
    wt = jnp.concatenate([w_in[:, c2:c3], w_in[:, c4:], w_g], axis=1).T.astype(BF16)
    conv_w8 = jnp.pad(conv_w, ((0, SUBLANES - CONV_WIDTH), (0, 0)))
    gate_par = jnp.stack([a_log.reshape(-1), dt_bias.reshape(-1)])
    prow = jnp.pad(gate_par, ((0, SUBLANES - 2), (2 * HEADS, LANES - 4 * HEADS)))
    pcol = jnp.pad(gate_par.T, ((2 * HEADS, 0), (0, LANES - 2)))
    return dict(
        wn=wn, wt=wt, conv_w8=conv_w8, prow=prow, pcol=pcol,
        gdn_g=gdn_norm_g.reshape(1, HEAD_DIM), lam_qk=lam_qk, diff_g=diff_norm_g.reshape(1, HEAD_DIM),
        w_out=w_out.astype(BF16), ln1_g=ln1_g.reshape(1, D_MODEL), ln1_b=ln1_b.reshape(1, D_MODEL),
        wg=w_gate_up[:, :D_FF].astype(BF16), wu=w_gate_up[:, D_FF:].astype(BF16), wd=w_down.astype(BF16),
        ln2_g=ln2_g.reshape(1, D_MODEL), ln2_b=ln2_b.reshape(1, D_MODEL))


def _trunk(x, layers, alpha):
    batch, seq, _ = x.shape
    rope = _rope_tables(seq)
    xf = x.reshape(batch * seq, D_MODEL)
    for l, p in enumerate(layers):
        lam_init = 0.8 - 0.6 * math.exp(-0.3 * l)
        qkv, z, gates, gates_t, qt, k, vt = _inproj(xf, p["wn"], p["wt"], rope, seq)
        local = _gdn_local(qkv, gates, gates_t, p["conv_w8"], p["prow"], p["pcol"], seq)
        o_f, o_b = _gdn_scan(local, batch, seq)
        o_attn = _attention(qt, k, vt, p["lam_qk"], p["diff_g"], batch, seq, lam_init)
        xf = _outproj(o_f, o_b, z, o_attn, xf, p["w_out"], p["gdn_g"], p["ln1_g"], p["ln1_b"], alpha)
        xf = _ffn(xf, p["wg"], p["wu"], p["wd"], p["ln2_g"], p["ln2_b"], alpha)
    return xf.reshape(batch, seq, D_MODEL)


def kernel(x_prompt, x_sample, w_in, conv_w, a_log, dt_bias, gdn_norm_g, lam_qk, diff_norm_g, w_out,
           ln1_g, ln1_b, w_gate_up, w_down, ln2_g, ln2_b):
    depth = w_in.shape[0]
    alpha = (2 * depth) ** 0.25
    layers = [_prep_layer(w_in[l], conv_w[l], a_log[l], dt_bias[l], gdn_norm_g[l], lam_qk[l], diff_norm_g[l],
                          w_out[l], ln1_g[l], ln1_b[l], w_gate_up[l], w_down[l], ln2_g[l], ln2_b[l])
              for l in range(depth)]
    return (_trunk(x_prompt, layers, alpha), _trunk(x_sample, layers, alpha))
```

```python
import functools
import math

import jax
import jax.numpy as jnp
from jax import lax
from jax.experimental import pallas as pl
from jax.experimental.pallas import tpu as pltpu

F32 = jnp.float32
BF16 = jnp.bfloat16

D_MODEL = 1024
HEADS = 4
HEAD_DIM = 128
GDN_QKV = 3 * HEADS * HEAD_DIM
GDN_Z = HEADS * HEAD_DIM
GDN_GATES = 4 * HEADS
DIFF_DQK = 64
DIFF_W = HEADS * 2 * DIFF_DQK
CONV_WIDTH = 5
CONV_PAD = CONV_WIDTH // 2
CHUNK = 64
ROPE_THETA = 500000.0
ROPE_DIM = DIFF_DQK // 4
ROPE_HALF = ROPE_DIM // 2
D_FF = int(math.ceil(8 * D_MODEL / 3 / 256)) * 256

LANES = 128
SUBLANES = 8
VMEM_LIMIT = 56 * 1024 * 1024

ROW_TILE = 512
GDN_LOCAL_ROWS = 256
GDN_SCAN_CHUNKS = 8
ATTN_Q_BLOCK = 256
ATTN_K_BLOCK = 256
ATTN_SUM_ROWS = 16
LOG2E = 1.4426950408889634


def _params(*sem):
    return pltpu.CompilerParams(dimension_semantics=sem, vmem_limit_bytes=VMEM_LIMIT)


def _sigmoid(x):
    return 1.0 / (1.0 + jnp.exp(-x))


def _softplus(x):
    return jnp.maximum(x, 0.0) + jnp.log(1.0 + jnp.exp(-jnp.abs(x)))


def _dot(a, b):
    return jnp.dot(a, b, preferred_element_type=F32)


def _dot_nt(a, b):
    return lax.dot_general(a, b, (((1,), (1,)), ((), ())), preferred_element_type=F32)


def _dot_tn(a, b):
    return lax.dot_general(a, b, (((0,), (0,)), ((), ())), preferred_element_type=F32)


def _layer_norm(y, g, b):
    mu = jnp.mean(y, -1, keepdims=True)
    d = y - mu
    var = jnp.mean(d * d, -1, keepdims=True)
    return d * lax.rsqrt(var + 1e-5) * g + b


def _inproj_kernel(x_ref, wn_ref, wt_ref, cos_ref, sin_ref, kc_ref, ks1_ref, ks2_ref,
                   qkv_ref, z_ref, g_ref, gt_ref, qt_ref, k_ref, vt_ref):
    xb = x_ref[...].astype(BF16)
    hn = _dot(xb, wn_ref[...])
    qkv_ref[...] = hn[:, :GDN_QKV]
    z_ref[...] = hn[:, GDN_QKV:GDN_QKV + GDN_Z]
    g_ref[...] = hn[:, GDN_QKV + GDN_Z:GDN_QKV + GDN_Z + LANES]
    dk0 = GDN_QKV + GDN_Z + LANES
    kc, ks1, ks2 = kc_ref[...], ks1_ref[...], ks2_ref[...]
    for h in range(HEADS):
        kh = hn[:, dk0 + h * LANES:dk0 + (h + 1) * LANES]
        up = pltpu.roll(kh, LANES - ROPE_HALF, axis=1)
        dn = pltpu.roll(kh, ROPE_HALF, axis=1)
        k_ref[:, h * LANES:(h + 1) * LANES] = (kh * kc + up * ks1 + dn * ks2).astype(BF16)
    ht = _dot_nt(wt_ref[...], xb)
    vt_ref[...] = ht[DIFF_W:2 * DIFF_W].astype(BF16)
    gt_ref[...] = ht[2 * DIFF_W:2 * DIFF_W + GDN_GATES]
    cos, sin = cos_ref[...], sin_ref[...]
    pieces = []
    for c in range(DIFF_W // DIFF_DQK):
        r = c * DIFF_DQK
        a, b = ht[r:r + ROPE_HALF], ht[r + ROPE_HALF:r + ROPE_DIM]
        pieces += [a * cos - b * sin, b * cos + a * sin, ht[r + ROPE_DIM:r + DIFF_DQK]]
    qt_ref[...] = (jnp.concatenate(pieces, axis=0) * (DIFF_DQK ** -0.5 * LOG2E)).astype(BF16)


def _inproj(x, wn, wt, rope, seq):
    t = x.shape[0]
    tm = ROW_TILE
    nps = seq // tm
    cos_t, sin_t, kc, ks1, ks2 = rope
    row = lambda i: (i, 0)
    col = lambda i: (0, i)
    const = lambda i: (0, 0)
    pos_row = lambda i: (i % nps, 0)
    pos_col = lambda i: (0, i % nps)
    n_cols = wn.shape[1]
    t_rows = wt.shape[0]
    return pl.pallas_call(
        _inproj_kernel,
        grid=(t // tm,),
        in_specs=[
            pl.BlockSpec((tm, D_MODEL), row),
            pl.BlockSpec((D_MODEL, n_cols), const),
            pl.BlockSpec((t_rows, D_MODEL), const),
            pl.BlockSpec((ROPE_HALF, tm), pos_col),
            pl.BlockSpec((ROPE_HALF, tm), pos_col),
            pl.BlockSpec((tm, LANES), pos_row),
            pl.BlockSpec((tm, LANES), pos_row),
            pl.BlockSpec((tm, LANES), pos_row),
        ],
        out_specs=[
            pl.BlockSpec((tm, GDN_QKV), row),
            pl.BlockSpec((tm, GDN_Z), row),
            pl.BlockSpec((tm, LANES), row),
            pl.BlockSpec((GDN_GATES, tm), col),
            pl.BlockSpec((DIFF_W, tm), col),
            pl.BlockSpec((tm, DIFF_W), row),
            pl.BlockSpec((DIFF_W, tm), col),
        ],
        out_shape=[
            jax.ShapeDtypeStruct((t, GDN_QKV), F32),
            jax.ShapeDtypeStruct((t, GDN_Z), F32),
            jax.ShapeDtypeStruct((t, LANES), F32),
            jax.ShapeDtypeStruct((GDN_GATES, t), F32),
            jax.ShapeDtypeStruct((DIFF_W, t), BF16),
            jax.ShapeDtypeStruct((t, DIFF_W), BF16),
            jax.ShapeDtypeStruct((DIFF_W, t), BF16),
        ],
        compiler_params=_params("parallel"),
        name="inproj",
    )(x, wn, wt, cos_t, sin_t, kc, ks1, ks2)


def _unit_tri_inverse(m, eye, same16, mid32, same32):
    b16 = lambda a: a.astype(BF16)
    d = b16(jnp.where(same16, m, 0.0))
    c32 = b16(jnp.where(mid32, m, 0.0))
    c64 = b16(jnp.where(same32, 0.0, m))
    d2 = b16(_dot(d, d))
    d4 = b16(_dot(d2, d2))
    d8 = b16(_dot(d4, d4))
    t = eye - d
    t = t + _dot(b16(t), d2)
    t = t + _dot(b16(t), d4)
    t = t + _dot(b16(t), d8)
    tb = b16(t)
    t = t - _dot(tb, b16(_dot(c32, tb)))
    tb = b16(t)
    return t - _dot(tb, b16(_dot(c64, tb)))


def _gdn_local_kernel(qkv_ref, prev_ref, next_ref, g_ref, gt_ref, cw_ref, prow_ref, pcol_ref,
                      uf_ref, wf_ref, qdf_ref, kdf_ref, af_ref,
                      ub_ref, wb_ref, qdb_ref, kdb_ref, ab_ref, gl_ref, *, blocks_per_seq):
    sb = qkv_ref.shape[0]
    j = pl.program_id(0) % blocks_per_seq
    prev_on = j > 0
    next_on = j < blocks_per_seq - 1
    ext_rows = sb + 2 * SUBLANES

    conv = []
    for cg in range(GDN_QKV // LANES):
        sl = slice(cg * LANES, (cg + 1) * LANES)
        ext = jnp.concatenate([jnp.where(prev_on, prev_ref[:, sl], 0.0), qkv_ref[:, sl],
                               jnp.where(next_on, next_ref[:, sl], 0.0)], axis=0)
        acc = None
        for tap in range(CONV_WIDTH):
            shift = (CONV_PAD - tap) % ext_rows
            rolled = ext if shift == 0 else pltpu.roll(ext, shift, axis=0)
            term = rolled[SUBLANES:SUBLANES + sb] * cw_ref[tap:tap + 1, sl]
            acc = term if acc is None else acc + term
        conv.append(acc * _sigmoid(acc))

    def l2n(t):
        return t * lax.rsqrt(jnp.sum(t * t, -1, keepdims=True) + 1e-6)

    g_c = g_ref[...]
    beta_c = _sigmoid(g_c)
    gval_c = -jnp.exp(prow_ref[0:1, :]) * _softplus(g_c + prow_ref[1:2, :])
    g_r = gt_ref[...]
    gval_r = -jnp.exp(pcol_ref[:, 0:1]) * _softplus(g_r + pcol_ref[:, 1:2])

    ri = lax.broadcasted_iota(jnp.int32, (sb, sb), 0)
    ci = lax.broadcasted_iota(jnp.int32, (sb, sb), 1)
    same64 = (ri // CHUNK) == (ci // CHUNK)
    same32 = (ri // 32) == (ci // 32)
    same16 = (ri // 16) == (ci // 16)
    mid32 = jnp.logical_and(same32, jnp.logical_not(same16))
    diag = ri == ci
    eye = jnp.where(diag, 1.0, 0.0).astype(F32)
    incl = (jnp.logical_and(same64, ci <= ri), jnp.logical_and(same64, ci >= ri))

    low = jnp.where(incl[0], 1.0, 0.0).astype(F32)
    upp = jnp.where(incl[1], 1.0, 0.0).astype(F32)
    blk = jnp.where(same64, 1.0, 0.0).astype(F32)
    hdot = functools.partial(jnp.dot, preferred_element_type=F32, precision=lax.Precision.HIGHEST)
    gc_c = (hdot(low, gval_c), hdot(upp, gval_c))
    gc_r = (hdot(gval_r, upp), hdot(gval_r, low))
    gtot_c = hdot(blk, gval_c)
    gtot_r = hdot(gval_r, blk)
    for c in range(sb // CHUNK):
        gl = jnp.exp(gtot_r[SUBLANES:2 * SUBLANES, c * CHUNK:c * CHUNK + 1])
        gl_ref[c * SUBLANES:(c + 1) * SUBLANES, :] = jnp.broadcast_to(gl, (SUBLANES, LANES))

    u_refs, w_refs, qd_refs, kd_refs, a_refs = ((uf_ref, ub_ref), (wf_ref, wb_ref), (qdf_ref, qdb_ref),
                                                (kdf_ref, kdb_ref), (af_ref, ab_ref))
    for h in range(HEADS):
        hs = slice(h * HEAD_DIM, (h + 1) * HEAD_DIM)
        qh = l2n(conv[h]) * (HEAD_DIM ** -0.5)
        kh = l2n(conv[HEADS + h])
        vh = conv[2 * HEADS + h]
        qb, kb = qh.astype(BF16), kh.astype(BF16)
        kk = _dot_nt(kb, kb)
        qk = _dot_nt(qb, kb)
        for d in range(2):
            lane = 2 * HEADS + HEADS * d + h
            gcol = gc_c[d][:, lane:lane + 1]
            grow = gc_r[d][lane:lane + 1, :]
            bcol = beta_c[:, HEADS * d + h:HEADS * d + h + 1]
            decay = jnp.exp(jnp.where(incl[d], gcol - grow, -jnp.inf))
            m = jnp.where(diag, 0.0, kk * bcol * decay)
            tinv = _unit_tri_inverse(m, eye, same16, mid32, same32)
            eg = jnp.exp(gcol)
            rhs = jnp.concatenate([vh * bcol, kh * (bcol * eg)], axis=1).astype(BF16)
            sol = _dot(tinv.astype(BF16), rhs)
            u_refs[d][:, hs] = sol[:, :HEAD_DIM]
            w_refs[d][:, hs] = sol[:, HEAD_DIM:].astype(BF16)
            qd_refs[d][:, hs] = (qh * eg).astype(BF16)
            kd_refs[d][:, hs] = (kh * jnp.exp(gtot_c[:, lane:lane + 1] - gcol)).astype(BF16)
            attn = (qk * decay).astype(BF16)
            for c in range(sb // CHUNK):
                rows = slice(c * CHUNK, (c + 1) * CHUNK)
                a_refs[d][h, rows, :] = attn[rows, rows]


def _gdn_local(qkv, gates, gates_t, conv_w8, prow, pcol, seq):
    t = qkv.shape[0]
    sb = GDN_LOCAL_ROWS
    bps = seq // sb
    hb = sb // SUBLANES
    n8 = t // SUBLANES
    row = lambda i: (i, 0)
    const = lambda i: (0, 0)
    wide = jax.ShapeDtypeStruct((t, HEADS * HEAD_DIM), F32)
    wide_bf = jax.ShapeDtypeStruct((t, HEADS * HEAD_DIM), BF16)
    attn = jax.ShapeDtypeStruct((HEADS, t, CHUNK), BF16)
    wide_spec = pl.BlockSpec((sb, HEADS * HEAD_DIM), row)
    attn_spec = pl.BlockSpec((HEADS, sb, CHUNK), lambda i: (0, i, 0))
    dir_shapes = [wide, wide_bf, wide_bf, wide_bf, attn]
    dir_specs = [wide_spec, wide_spec, wide_spec, wide_spec, attn_spec]
    return pl.pallas_call(
        functools.partial(_gdn_local_kernel, blocks_per_seq=bps),
        grid=(t // sb,),
        in_specs=[
            pl.BlockSpec((sb, GDN_QKV), row),
            pl.BlockSpec((SUBLANES, GDN_QKV), lambda i: (jnp.maximum(i * hb - 1, 0), 0)),
            pl.BlockSpec((SUBLANES, GDN_QKV), lambda i: (jnp.minimum((i + 1) * hb, n8 - 1), 0)),
            pl.BlockSpec((sb, LANES), row),
            pl.BlockSpec((GDN_GATES, sb), lambda i: (0, i)),
            pl.BlockSpec((SUBLANES, GDN_QKV), const),
            pl.BlockSpec((SUBLANES, LANES), const),
            pl.BlockSpec((GDN_GATES, LANES), const),
        ],
        out_specs=dir_specs + dir_specs + [pl.BlockSpec((sb // CHUNK * SUBLANES, LANES), row)],
        out_shape=dir_shapes + dir_shapes + [jax.ShapeDtypeStruct((t // CHUNK * SUBLANES, LANES), F32)],
        compiler_params=_params("parallel"),
        name="gdn_local",
    )(qkv, qkv, qkv, gates, gates_t, conv_w8, prow, pcol)


def _gdn_scan_kernel(uf_ref, wf_ref, qdf_ref, kdf_ref, af_ref, glf_ref,
                     ub_ref, wb_ref, qdb_ref, kdb_ref, ab_ref, glb_ref,
                     of_ref, ob_ref, state_ref, *, chunks):
    @pl.when(pl.program_id(1) == 0)
    def _():
        state_ref[...] = jnp.zeros_like(state_ref)

    dirs = ((uf_ref, wf_ref, qdf_ref, kdf_ref, af_ref, glf_ref, of_ref),
            (ub_ref, wb_ref, qdb_ref, kdb_ref, ab_ref, glb_ref, ob_ref))

    def body(c, carry):
        for d in range(2):
            u_ref, w_ref, qd_ref, kd_ref, a_ref, gl_ref, o_ref = dirs[d]
            ch = c if d == 0 else chunks - 1 - c
            rows = pl.ds(pl.multiple_of(ch * CHUNK, CHUNK), CHUNK)
            for h in range(HEADS):
                hs = slice(h * HEAD_DIM, (h + 1) * HEAD_DIM)
                s = state_ref[d * HEADS + h]
                sb16 = s.astype(BF16)
                v_new = u_ref[rows, hs] - _dot(w_ref[rows, hs], sb16)
                vb = v_new.astype(BF16)
                o_ref[rows, hs] = _dot(qd_ref[rows, hs], sb16) + _dot(a_ref[h, rows, :], vb)
                gl = gl_ref[pl.ds(ch * SUBLANES + d * HEADS + h, 1), :]
                state_ref[d * HEADS + h] = s * gl + _dot_tn(kd_ref[rows, hs], vb)
        return carry

    lax.fori_loop(0, chunks, body, 0)


def _gdn_scan(local_out, batch, seq):
    uf, wf, qdf, kdf, af, ub, wb, qdb, kdb, ab, gl = local_out
    t = uf.shape[0]
    cb = GDN_SCAN_CHUNKS
    rb = cb * CHUNK
    nb = seq // rb
    fwd = lambda b, j: (b * nb + j, 0)
    bwd = lambda b, j: (b * nb + nb - 1 - j, 0)
    fwd3 = lambda b, j: (0, b * nb + j, 0)
    bwd3 = lambda b, j: (0, b * nb + nb - 1 - j, 0)

    def dir_specs(m2, m3):
        wide = pl.BlockSpec((rb, HEADS * HEAD_DIM), m2)
        return [wide, wide, wide, wide, pl.BlockSpec((HEADS, rb, CHUNK), m3),
                pl.BlockSpec((cb * SUBLANES, LANES), m2)]

    out = jax.ShapeDtypeStruct((t, HEADS * HEAD_DIM), F32)
    return pl.pallas_call(
        functools.partial(_gdn_scan_kernel, chunks=cb),
        grid=(batch, nb),
        in_specs=dir_specs(fwd, fwd3) + dir_specs(bwd, bwd3),
        out_specs=[pl.BlockSpec((rb, HEADS * HEAD_DIM), fwd), pl.BlockSpec((rb, HEADS * HEAD_DIM), bwd)],
        out_shape=[out, out],
        scratch_shapes=[pltpu.VMEM((2 * HEADS, HEAD_DIM, HEAD_DIM), F32)],
        compiler_params=_params("parallel", "arbitrary"),
        name="gdn_scan",
    )(uf, wf, qdf, kdf, af, gl, ub, wb, qdb, kdb, ab, gl)


def _attn_kernel(qt_ref, k_ref, vt_ref, lq_ref, g_ref, o_ref, s_ref, p_ref, acc_ref, *, seq, kblk, lam_init):
    qb = qt_ref.shape[1]
    nblk = seq // kblk
    qt = qt_ref[...]
    row = lax.broadcasted_iota(jnp.int32, (2 * DIFF_DQK, 2 * qb), 0)
    colq = lax.broadcasted_iota(jnp.int32, (2 * DIFF_DQK, 2 * qb), 1)
    keep = (row < DIFF_DQK) == (colq < qb)
    rhs = jnp.where(keep, jnp.concatenate([qt, qt], axis=1), jnp.zeros((), BF16))
    ones = jnp.ones((ATTN_SUM_ROWS, kblk), BF16)

    def key_rows(i):
        return pl.ds(i * kblk if isinstance(i, int) else pl.multiple_of(i * kblk, kblk), kblk)

    def scores(i):
        return _dot(k_ref[key_rows(i), :], rhs)

    def weighted_values(i, slot):
        lhs = jnp.concatenate([vt_ref[:, key_rows(i)], ones], axis=0)
        return _dot(lhs, p_ref[slot])

    def step(i, slot, m, first=False, last=False):
        if not first:
            pv = weighted_values(i - 1, 1 - slot)
        if not last:
            s_ref[1 - slot] = scores(i + 1)
        s = s_ref[slot]
        m_new = jnp.maximum(m, jnp.max(s, axis=0, keepdims=True))
        p_ref[slot] = jnp.exp2(s - m_new).astype(BF16)
        if not first:
            acc_ref[...] = (acc_ref[...] + pv) * jnp.exp2(m - m_new)
        return m_new

    s_ref[0] = scores(0)
    acc_ref[...] = jnp.zeros_like(acc_ref)
    m = step(0, 0, jnp.full((1, 2 * qb), -jnp.inf, F32), first=True)

    def pair(t, m):
        m = step(2 * t + 1, 1, m)
        return step(2 * t + 2, 0, m)

    m = lax.fori_loop(0, (nblk - 2) // 2, pair, m)
    step(nblk - 1, 1, m, last=True)
    acc = acc_ref[...] + weighted_values(nblk - 1, 1)

    lq = lq_ref[...]
    lam = (jnp.exp(jnp.sum(lq[0:1] * lq[1:2], axis=-1, keepdims=True))
           - jnp.exp(jnp.sum(lq[2:3] * lq[3:4], axis=-1, keepdims=True)) + lam_init)
    on = acc[:HEAD_DIM] / acc[HEAD_DIM:HEAD_DIM + 1]
    o = (on[:, :qb] - lam * on[:, qb:]).T
    o = o * lax.rsqrt(jnp.mean(o * o, -1, keepdims=True) + 1e-6) * g_ref[...]
    o_ref[...] = o * (1.0 - lam_init)


def _attention(qt, k, vt, lam_qk, norm_g, batch, seq, lam_init):
    t = k.shape[0]
    qb = ATTN_Q_BLOCK
    kblk = ATTN_K_BLOCK
    nq = seq // qb
    assert seq % (2 * kblk) == 0
    return pl.pallas_call(
        functools.partial(_attn_kernel, seq=seq, kblk=kblk, lam_init=lam_init),
        grid=(batch, HEADS, nq),
        in_specs=[
            pl.BlockSpec((2 * DIFF_DQK, qb), lambda b, h, i: (h, b * nq + i)),
            pl.BlockSpec((seq, 2 * DIFF_DQK), lambda b, h, i: (b, h)),
            pl.BlockSpec((HEAD_DIM, seq), lambda b, h, i: (h, b)),
            pl.BlockSpec((4, DIFF_DQK), lambda b, h, i: (0, 0)),
            pl.BlockSpec((1, HEAD_DIM), lambda b, h, i: (0, 0)),
        ],
        out_specs=pl.BlockSpec((qb, HEAD_DIM), lambda b, h, i: (b * nq + i, h)),
        out_shape=jax.ShapeDtypeStruct((t, HEADS * HEAD_DIM), F32),
        scratch_shapes=[pltpu.VMEM((2, kblk, 2 * qb), F32),
                        pltpu.VMEM((2, kblk, 2 * qb), BF16),
                        pltpu.VMEM((HEAD_DIM + ATTN_SUM_ROWS, 2 * qb), F32)],
        compiler_params=_params("parallel", "parallel", "arbitrary"),
        name="diff_attn",
    )(qt, k, vt, lam_qk, norm_g)


def _outproj_kernel(of_ref, ob_ref, z_ref, oa_ref, x_ref, w_ref, gg_ref, lg_ref, lb_ref, y_ref, *, alpha):
    parts = []
    for h in range(HEADS):
        hs = slice(h * HEAD_DIM, (h + 1) * HEAD_DIM)
        o = of_ref[:, hs] + ob_ref[:, hs]
        o = o * lax.rsqrt(jnp.mean(o * o, -1, keepdims=True) + 1e-6) * gg_ref[...]
        zz = z_ref[:, hs]
        parts.append(o * (zz * _sigmoid(zz)))
    mixed = jnp.concatenate(parts + [oa_ref[...]], axis=1).astype(BF16)
    y = alpha * x_ref[...] + _dot(mixed, w_ref[...])
    y_ref[...] = _layer_norm(y, lg_ref[...], lb_ref[...])


def _outproj(o_f, o_b, z, o_attn, x, w_out, gdn_g, ln_g, ln_b, alpha):
    t = x.shape[0]
    tm = ROW_TILE
    row = lambda i: (i, 0)
    const = lambda i: (0, 0)
    half = pl.BlockSpec((tm, HEADS * HEAD_DIM), row)
    return pl.pallas_call(
        functools.partial(_outproj_kernel, alpha=alpha),
        grid=(t // tm,),
        in_specs=[half, half, half, half,
                  pl.BlockSpec((tm, D_MODEL), row),
                  pl.BlockSpec((D_MODEL, D_MODEL), const),
                  pl.BlockSpec((1, HEAD_DIM), const),
                  pl.BlockSpec((1, D_MODEL), const),
                  pl.BlockSpec((1, D_MODEL), const)],
        out_specs=pl.BlockSpec((tm, D_MODEL), row),
        out_shape=jax.ShapeDtypeStruct((t, D_MODEL), F32),
        compiler_params=_params("parallel"),
        name="outproj_ln",
    )(o_f, o_b, z, o_attn, x, w_out, gdn_g, ln_g, ln_b)


def _ffn_kernel(x_ref, wg_ref, wu_ref, wd_ref, lg_ref, lb_ref, y_ref, *, alpha):
    x = x_ref[...]
    xb = x.astype(BF16)
    gate = _dot(xb, wg_ref[...])
    up = _dot(xb, wu_ref[...])
    act = (gate * _sigmoid(gate) * up).astype(BF16)
    y = alpha * x + _dot(act, wd_ref[...])
    y_ref[...] = _layer_norm(y, lg_ref[...], lb_ref[...])


def _ffn(x, wg, wu, wd, ln_g, ln_b, alpha):
    t = x.shape[0]
    tm = ROW_TILE // 2
    row = lambda i: (i, 0)
    const = lambda i: (0, 0)
    once = pl.Buffered(1)
    return pl.pallas_call(
        functools.partial(_ffn_kernel, alpha=alpha),
        grid=(t // tm,),
        in_specs=[pl.BlockSpec((tm, D_MODEL), row),
                  pl.BlockSpec((D_MODEL, D_FF), const, pipeline_mode=once),
                  pl.BlockSpec((D_MODEL, D_FF), const, pipeline_mode=once),
                  pl.BlockSpec((D_FF, D_MODEL), const, pipeline_mode=once),
                  pl.BlockSpec((1, D_MODEL), const),
                  pl.BlockSpec((1, D_MODEL), const)],
        out_specs=pl.BlockSpec((tm, D_MODEL), row),
        out_shape=jax.ShapeDtypeStruct((t, D_MODEL), F32),
        compiler_params=_params("parallel"),
        name="swiglu_ln",
    )(x, wg, wu, wd, ln_g, ln_b)


def _rope_tables(seq):
    inv = 1.0 / (ROPE_THETA ** (jnp.arange(0, ROPE_DIM, 2, dtype=F32) / ROPE_DIM))
    ang = jnp.arange(seq, dtype=F32)[:, None] * inv[None, :]
    cos, sin = jnp.cos(ang), jnp.sin(ang)
    zero8 = jnp.zeros_like(sin)
    rest = DIFF_DQK - ROPE_DIM
    c64 = jnp.concatenate([cos, cos, jnp.ones((seq, rest), F32)], axis=1)
    s1 = jnp.concatenate([-sin, zero8, jnp.zeros((seq, rest), F32)], axis=1)
    s2 = jnp.concatenate([zero8, sin, jnp.zeros((seq, rest), F32)], axis=1)
    two = lambda a: jnp.concatenate([a, a], axis=1)
    return cos.T, sin.T, two(c64), two(s1), two(s2)


def _prep_layer(w_in, conv_w, a_log, dt_bias, gdn_norm_g, lam_qk, diff_norm_g, w_out, ln1_g, ln1_b,
                w_gate_up, w_down, ln2_g, ln2_b):
    c0 = GDN_QKV
    c1 = c0 + GDN_Z
    c2 = c1 + GDN_GATES
    c3 = c2 + DIFF_W
    c4 = c3 + DIFF_W
    w_g = w_in[:, c1:c2]
    w_g_pad = jnp.pad(w_g, ((0, 0), (0, LANES - GDN_GATES)))
    wn = jnp.concatenate([w_in[:, :c1], w_g_pad, w_in[:, c3:c4]], axis=1).astype(BF16)
    wt = jnp.concatenate([w_in[:, c2:c3], w_in[:, c4:], w_g], axis=1).T.astype(BF16)
    conv_w8 = jnp.pad(conv_w, ((0, SUBLANES - CONV_WIDTH), (0, 0)))
    gate_par = jnp.stack([a_log.reshape(-1), dt_bias.reshape(-1)])
    prow = jnp.pad(gate_par, ((0, SUBLANES - 2), (2 * HEADS, LANES - 4 * HEADS)))
    pcol = jnp.pad(gate_par.T, ((2 * HEADS, 0), (0, LANES - 2)))
    return dict(
        wn=wn, wt=wt, conv_w8=conv_w8, prow=prow, pcol=pcol,
        gdn_g=gdn_norm_g.reshape(1, HEAD_DIM), lam_qk=lam_qk, diff_g=diff_norm_g.reshape(1, HEAD_DIM),
        w_out=w_out.astype(BF16), ln1_g=ln1_g.reshape(1, D_MODEL), ln1_b=ln1_b.reshape(1, D_MODEL),
        wg=w_gate_up[:, :D_FF].astype(BF16), wu=w_gate_up[:, D_FF:].astype(BF16), wd=w_down.astype(BF16),
        ln2_g=ln2_g.reshape(1, D_MODEL), ln2_b=ln2_b.reshape(1, D_MODEL))


def _trunk(x, layers, alpha):
    batch, seq, _ = x.shape
    rope = _rope_tables(seq)
    xf = x.reshape(batch * seq, D_MODEL)
    for l, p in enumerate(layers):
        lam_init = 0.8 - 0.6 * math.exp(-0.3 * l)
        qkv, z, gates, gates_t, qt, k, vt = _inproj(xf, p["wn"], p["wt"], rope, seq)
        local = _gdn_local(qkv, gates, gates_t, p["conv_w8"], p["prow"], p["pcol"], seq)
        o_f, o_b = _gdn_scan(local, batch, seq)
        o_attn = _attention(qt, k, vt, p["lam_qk"], p["diff_g"], batch, seq, lam_init)
        xf = _outproj(o_f, o_b, z, o_attn, xf, p["w_out"], p["gdn_g"], p["ln1_g"], p["ln1_b"], alpha)
        xf = _ffn(xf, p["wg"], p["wu"], p["wd"], p["ln2_g"], p["ln2_b"], alpha)
    return xf.reshape(batch, seq, D_MODEL)


def kernel(x_prompt, x_sample, w_in, conv_w, a_log, dt_bias, gdn_norm_g, lam_qk, diff_norm_g, w_out,
           ln1_g, ln1_b, w_gate_up, w_down, ln2_g, ln2_b):
    depth = w_in.shape[0]
    alpha = (2 * depth) ** 0.25
    layers = [_prep_layer(w_in[l], conv_w[l], a_log[l], dt_bias[l], gdn_norm_g[l], lam_qk[l], diff_norm_g[l],
                          w_out[l], ln1_g[l], ln1_b[l], w_gate_up[l], w_down[l], ln2_g[l], ln2_b[l])
              for l in range(depth)]
    return (_trunk(x_prompt, layers, alpha), _trunk(x_sample, layers, alpha))
```

```python
import functools
import math

import jax
import jax.numpy as jnp
from jax import lax
from jax.experimental import pallas as pl
from jax.experimental.pallas import tpu as pltpu

F32 = jnp.float32
BF16 = jnp.bfloat16

D_MODEL = 1024
HEADS = 4
HEAD_DIM = 128
GDN_QKV = 3 * HEADS * HEAD_DIM
GDN_Z = HEADS * HEAD_DIM
GDN_GATES = 4 * HEADS
DIFF_DQK = 64
DIFF_W = HEADS * 2 * DIFF_DQK
CONV_WIDTH = 5
CONV_PAD = CONV_WIDTH // 2
CHUNK = 64
ROPE_THETA = 500000.0
ROPE_DIM = DIFF_DQK // 4
ROPE_HALF = ROPE_DIM // 2
D_FF = int(math.ceil(8 * D_MODEL / 3 / 256)) * 256

LANES = 128
SUBLANES = 8
VMEM_LIMIT = 56 * 1024 * 1024

ROW_TILE = 512
GDN_LOCAL_ROWS = 256
GDN_SCAN_CHUNKS = 4
GDN_SCAN_GROUP = 4
ATTN_Q_BLOCK = 256
ATTN_AHEAD = 2
ATTN_S_SLOTS = 4
ATTN_K_BLOCK = 256
ATTN_SUM_ROWS = 16
LOG2E = 1.4426950408889634


def _params(*sem):
    return pltpu.CompilerParams(dimension_semantics=sem, vmem_limit_bytes=VMEM_LIMIT)


def _sigmoid(x):
    return 1.0 / (1.0 + jnp.exp(-x))


def _softplus(x):
    return jnp.maximum(x, 0.0) + jnp.log(1.0 + jnp.exp(-jnp.abs(x)))


def _dot(a, b):
    return jnp.dot(a, b, preferred_element_type=F32)


def _dot_nt(a, b):
    return lax.dot_general(a, b, (((1,), (1,)), ((), ())), preferred_element_type=F32)


def _dot_tn(a, b):
    return lax.dot_general(a, b, (((0,), (0,)), ((), ())), preferred_element_type=F32)


def _layer_norm(y, g, b):
    mu = jnp.mean(y, -1, keepdims=True)
    d = y - mu
    var = jnp.mean(d * d, -1, keepdims=True)
    return d * lax.rsqrt(var + 1e-5) * g + b


def _inproj_kernel(x_ref, wn_ref, wt_ref, cos_ref, sin_ref, kc_ref, ks1_ref, ks2_ref,
                   qkv_ref, z_ref, gt_ref, qt_ref, k_ref, vt_ref):
    xb = x_ref[...].astype(BF16)
    hn = _dot(xb, wn_ref[...])
    qkv_ref[...] = hn[:, :GDN_QKV]
    z_ref[...] = hn[:, GDN_QKV:GDN_QKV + GDN_Z]
    dk0 = GDN_QKV + GDN_Z
    kc, ks1, ks2 = kc_ref[...], ks1_ref[...], ks2_ref[...]
    for h in range(HEADS):
        kh = hn[:, dk0 + h * LANES:dk0 + (h + 1) * LANES]
        up = pltpu.roll(kh, LANES - ROPE_HALF, axis=1)
        dn = pltpu.roll(kh, ROPE_HALF, axis=1)
        k_ref[:, h * LANES:(h + 1) * LANES] = (kh * kc + up * ks1 + dn * ks2).astype(BF16)
    ht = _dot_nt(wt_ref[...], xb)
    vt_ref[...] = ht[DIFF_W:2 * DIFF_W].astype(BF16)
    gt_ref[...] = ht[2 * DIFF_W:2 * DIFF_W + GDN_GATES]
    cos, sin = cos_ref[...], sin_ref[...]
    pieces = []
    for c in range(DIFF_W // DIFF_DQK):
        r = c * DIFF_DQK
        a, b = ht[r:r + ROPE_HALF], ht[r + ROPE_HALF:r + ROPE_DIM]
        pieces += [a * cos - b * sin, b * cos + a * sin, ht[r + ROPE_DIM:r + DIFF_DQK]]
    qt_ref[...] = (jnp.concatenate(pieces, axis=0) * (DIFF_DQK ** -0.5 * LOG2E)).astype(BF16)


def _inproj(x, wn, wt, rope, seq):
    t = x.shape[0]
    tm = ROW_TILE
    nps = seq // tm
    cos_t, sin_t, kc, ks1, ks2 = rope
    row = lambda i: (i, 0)
    col = lambda i: (0, i)
    const = lambda i: (0, 0)
    pos_row = lambda i: (i % nps, 0)
    pos_col = lambda i: (0, i % nps)
    n_cols = wn.shape[1]
    t_rows = wt.shape[0]
    return pl.pallas_call(
        _inproj_kernel,
        grid=(t // tm,),
        in_specs=[
            pl.BlockSpec((tm, D_MODEL), row),
            pl.BlockSpec((D_MODEL, n_cols), const),
            pl.BlockSpec((t_rows, D_MODEL), const),
            pl.BlockSpec((ROPE_HALF, tm), pos_col),
            pl.BlockSpec((ROPE_HALF, tm), pos_col),
            pl.BlockSpec((tm, LANES), pos_row),
            pl.BlockSpec((tm, LANES), pos_row),
            pl.BlockSpec((tm, LANES), pos_row),
        ],
        out_specs=[
            pl.BlockSpec((tm, GDN_QKV), row),
            pl.BlockSpec((tm, GDN_Z), row),
            pl.BlockSpec((GDN_GATES, tm), col),
            pl.BlockSpec((DIFF_W, tm), col),
            pl.BlockSpec((tm, DIFF_W), row),
            pl.BlockSpec((DIFF_W, tm), col),
        ],
        out_shape=[
            jax.ShapeDtypeStruct((t, GDN_QKV), F32),
            jax.ShapeDtypeStruct((t, GDN_Z), F32),
            jax.ShapeDtypeStruct((GDN_GATES, t), F32),
            jax.ShapeDtypeStruct((DIFF_W, t), BF16),
            jax.ShapeDtypeStruct((t, DIFF_W), BF16),
            jax.ShapeDtypeStruct((DIFF_W, t), BF16),
        ],
        compiler_params=_params("parallel"),
        name="inproj",
    )(x, wn, wt, cos_t, sin_t, kc, ks1, ks2)


def _unit_tri_inverses(ms, eye, same16, mid32, same32):
    b16 = lambda a: a.astype(BF16)
    ds = [b16(jnp.where(same16, m, 0.0)) for m in ms]
    c32s = [b16(jnp.where(mid32, m, 0.0)) for m in ms]
    c64s = [b16(jnp.where(same32, 0.0, m)) for m in ms]
    d2s = [b16(_dot(d, d)) for d in ds]
    d4s = [b16(_dot(d2, d2)) for d2 in d2s]
    d8s = [b16(_dot(d4, d4)) for d4 in d4s]
    ts = [eye - d for d in ds]
    for powers in (d2s, d4s, d8s):
        ts = [t + _dot(b16(t), p) for t, p in zip(ts, powers)]
    for corners in (c32s, c64s):
        tbs = [b16(t) for t in ts]
        inner = [b16(_dot(c, tb)) for c, tb in zip(corners, tbs)]
        ts = [t - _dot(tb, i) for t, tb, i in zip(ts, tbs, inner)]
    return ts


def _gdn_local_kernel(qkv_ref, prev_ref, next_ref, gt_ref, cw_ref, pcol_ref,
                      uf_ref, wf_ref, qdf_ref, kdf_ref, af_ref,
                      ub_ref, wb_ref, qdb_ref, kdb_ref, ab_ref, gl_ref, *, blocks_per_seq):
    sb = qkv_ref.shape[0]
    j = pl.program_id(0) % blocks_per_seq
    prev_on = j > 0
    next_on = j < blocks_per_seq - 1
    ext_rows = sb + 2 * SUBLANES

    conv = []
    for cg in range(GDN_QKV // LANES):
        sl = slice(cg * LANES, (cg + 1) * LANES)
        ext = jnp.concatenate([jnp.where(prev_on, prev_ref[:, sl], 0.0), qkv_ref[:, sl],
                               jnp.where(next_on, next_ref[:, sl], 0.0)], axis=0)
        acc = None
        for tap in range(CONV_WIDTH):
            shift = (CONV_PAD - tap) % ext_rows
            rolled = ext if shift == 0 else pltpu.roll(ext, shift, axis=0)
            term = rolled[SUBLANES:SUBLANES + sb] * cw_ref[tap:tap + 1, sl]
            acc = term if acc is None else acc + term
        conv.append(acc * _sigmoid(acc))

    def l2n(t):
        return t * lax.rsqrt(jnp.sum(t * t, -1, keepdims=True) + 1e-6)

    nch = 2 * HEADS
    g_r = gt_ref[...]
    beta_r = _sigmoid(g_r[:nch])
    gval = -jnp.exp(pcol_ref[nch:, 0:1]) * _softplus(g_r[nch:] + pcol_ref[nch:, 1:2])

    pos = lax.broadcasted_iota(jnp.int32, (nch, sb), 1) % CHUNK
    pre, suf = gval, gval
    step = 1
    while step < CHUNK:
        pre = pre + jnp.where(pos >= step, pltpu.roll(pre, step, axis=1), 0.0)
        suf = suf + jnp.where(pos < CHUNK - step, pltpu.roll(suf, sb - step, axis=1), 0.0)
        step *= 2
    gtot = pre + suf - gval
    is_fwd = lax.broadcasted_iota(jnp.int32, (nch, sb), 0) < HEADS
    gc_r = jnp.where(is_fwd, pre, suf)
    for c in range(sb // CHUNK):
        gl = jnp.exp(gtot[:, c * CHUNK:c * CHUNK + 1])
        gl_ref[c * SUBLANES:(c + 1) * SUBLANES, :] = jnp.broadcast_to(gl, (SUBLANES, LANES))
    stack = jnp.concatenate([beta_r, gc_r, jnp.exp(gc_r), jnp.exp(gtot - gc_r),
                             jnp.zeros((LANES - 4 * nch, sb), F32)], axis=0)
    cols = stack.T

    ri = lax.broadcasted_iota(jnp.int32, (sb, sb), 0)
    ci = lax.broadcasted_iota(jnp.int32, (sb, sb), 1)
    same64 = (ri // CHUNK) == (ci // CHUNK)
    same32 = (ri // 32) == (ci // 32)
    same16 = (ri // 16) == (ci // 16)
    mid32 = jnp.logical_and(same32, jnp.logical_not(same16))
    diag = ri == ci
    eye = jnp.where(diag, 1.0, 0.0).astype(F32)
    incl = (jnp.logical_and(same64, ci <= ri), jnp.logical_and(same64, ci >= ri))

    u_refs, w_refs, qd_refs, kd_refs, a_refs = ((uf_ref, ub_ref), (wf_ref, wb_ref), (qdf_ref, qdb_ref),
                                                (kdf_ref, kdb_ref), (af_ref, ab_ref))
    qs = [l2n(conv[h]) * (HEAD_DIM ** -0.5) for h in range(HEADS)]
    ks = [l2n(conv[HEADS + h]) for h in range(HEADS)]
    vs = [conv[2 * HEADS + h] for h in range(HEADS)]
    kbs = [k.astype(BF16) for k in ks]
    kks = [_dot_nt(kb, kb) for kb in kbs]
    qks = [_dot_nt(q.astype(BF16), kb) for q, kb in zip(qs, kbs)]

    chains = [(h, d) for h in range(HEADS) for d in range(2)]
    col = lambda group, h, d: cols[:, group * nch + d * HEADS + h:group * nch + d * HEADS + h + 1]
    ms = []
    for h, d in chains:
        r = d * HEADS + h
        decay = jnp.exp(jnp.where(incl[d], col(1, h, d) - gc_r[r:r + 1, :], -jnp.inf))
        ms.append(jnp.where(diag, 0.0, kks[h] * col(0, h, d) * decay))
        attn = (qks[h] * decay).astype(BF16)
        for c in range(sb // CHUNK):
            rows = slice(c * CHUNK, (c + 1) * CHUNK)
            a_refs[d][h, rows, :] = attn[rows, rows]
    tinvs = _unit_tri_inverses(ms, eye, same16, mid32, same32)
    rhss = [jnp.concatenate([vs[h] * col(0, h, d), ks[h] * (col(0, h, d) * col(2, h, d))], axis=1).astype(BF16)
            for h, d in chains]
    sols = [_dot(t.astype(BF16), rhs) for t, rhs in zip(tinvs, rhss)]
    for (h, d), sol in zip(chains, sols):
        hs = slice(h * HEAD_DIM, (h + 1) * HEAD_DIM)
        u_refs[d][:, hs] = sol[:, :HEAD_DIM]
        w_refs[d][:, hs] = sol[:, HEAD_DIM:].astype(BF16)
        qd_refs[d][:, hs] = (qs[h] * col(2, h, d)).astype(BF16)
        kd_refs[d][:, hs] = (ks[h] * col(3, h, d)).astype(BF16)


def _gdn_local(qkv, gates_t, conv_w8, pcol, seq):
    t = qkv.shape[0]
    sb = GDN_LOCAL_ROWS
    bps = seq // sb
    hb = sb // SUBLANES
    n8 = t // SUBLANES
    row = lambda i: (i, 0)
    const = lambda i: (0, 0)
    wide = jax.ShapeDtypeStruct((t, HEADS * HEAD_DIM), F32)
    wide_bf = jax.ShapeDtypeStruct((t, HEADS * HEAD_DIM), BF16)
    attn = jax.ShapeDtypeStruct((HEADS, t, CHUNK), BF16)
    wide_spec = pl.BlockSpec((sb, HEADS * HEAD_DIM), row)
    attn_spec = pl.BlockSpec((HEADS, sb, CHUNK), lambda i: (0, i, 0))
    dir_shapes = [wide, wide_bf, wide_bf, wide_bf, attn]
    dir_specs = [wide_spec, wide_spec, wide_spec, wide_spec, attn_spec]
    return pl.pallas_call(
        functools.partial(_gdn_local_kernel, blocks_per_seq=bps),
        grid=(t // sb,),
        in_specs=[
            pl.BlockSpec((sb, GDN_QKV), row),
            pl.BlockSpec((SUBLANES, GDN_QKV), lambda i: (jnp.maximum(i * hb - 1, 0), 0)),
            pl.BlockSpec((SUBLANES, GDN_QKV), lambda i: (jnp.minimum((i + 1) * hb, n8 - 1), 0)),
            pl.BlockSpec((GDN_GATES, sb), lambda i: (0, i)),
            pl.BlockSpec((SUBLANES, GDN_QKV), const),
            pl.BlockSpec((GDN_GATES, LANES), const),
        ],
        out_specs=dir_specs + dir_specs + [pl.BlockSpec((sb // CHUNK * SUBLANES, LANES), row)],
        out_shape=dir_shapes + dir_shapes + [jax.ShapeDtypeStruct((t // CHUNK * SUBLANES, LANES), F32)],
        compiler_params=_params("parallel"),
        name="gdn_local",
    )(qkv, qkv, qkv, gates_t, conv_w8, pcol)


def _gdn_scan_kernel(uf_ref, wf_ref, qdf_ref, kdf_ref, af_ref, glf_ref,
                     ub_ref, wb_ref, qdb_ref, kdb_ref, ab_ref, glb_ref,
                     of_ref, ob_ref, state_ref, *, chunks, group):
    @pl.when(pl.program_id(1) == 0)
    def _():
        state_ref[...] = jnp.zeros_like(state_ref)

    dirs = ((uf_ref, wf_ref, qdf_ref, kdf_ref, af_ref, glf_ref, of_ref),
            (ub_ref, wb_ref, qdb_ref, kdb_ref, ab_ref, glb_ref, ob_ref))
    chains = [(g, d, h) for g in range(group) for d in range(2) for h in range(HEADS)]

    def body(c, carry):
        chunk = (c, chunks - 1 - c)
        rows = tuple(pl.ds(pl.multiple_of(ch * CHUNK, CHUNK), CHUNK) for ch in chunk)
        hs = lambda h: slice(h * HEAD_DIM, (h + 1) * HEAD_DIM)
        slot = lambda g, d, h: (g * 2 + d) * HEADS + h
        ws_qs = []
        for g, d, h in chains:
            w_ref, qd_ref = dirs[d][1], dirs[d][2]
            lhs = jnp.concatenate([w_ref[g, rows[d], hs(h)], qd_ref[g, rows[d], hs(h)]], axis=0)
            ws_qs.append(_dot(lhs, state_ref[slot(g, d, h)].astype(BF16)))
        vbs = [(dirs[d][0][g, rows[d], hs(h)] - wq[:CHUNK]).astype(BF16) for (g, d, h), wq in zip(chains, ws_qs)]
        for (g, d, h), wq, vb in zip(chains, ws_qs, vbs):
            kd_ref, a_ref, gl_ref, o_ref = dirs[d][3], dirs[d][4], dirs[d][5], dirs[d][6]
            o_ref[g, rows[d], hs(h)] = wq[CHUNK:] + _dot(a_ref[h, g, rows[d], :], vb)
            gl = gl_ref[g, pl.ds(chunk[d] * SUBLANES + d * HEADS + h, 1), :]
            state_ref[slot(g, d, h)] = state_ref[slot(g, d, h)] * gl + _dot_tn(kd_ref[g, rows[d], hs(h)], vb)
        return carry

    lax.fori_loop(0, chunks, body, 0)


def _gdn_scan(local_out, batch, seq):
    cb = GDN_SCAN_CHUNKS
    grp = GDN_SCAN_GROUP
    rb = cb * CHUNK
    nb = seq // rb
    wide = lambda a: a.reshape(batch, seq, HEADS * HEAD_DIM)
    uf, wf, qdf, kdf, af, ub, wb, qdb, kdb, ab, gl = local_out
    uf, wf, qdf, kdf, ub, wb, qdb, kdb = (wide(a) for a in (uf, wf, qdf, kdf, ub, wb, qdb, kdb))
    af, ab = (a.reshape(HEADS, batch, seq, CHUNK) for a in (af, ab))
    gl = gl.reshape(batch, seq // CHUNK * SUBLANES, LANES)
    fwd = lambda b, j: (b, j, 0)
    bwd = lambda b, j: (b, nb - 1 - j, 0)
    fwd4 = lambda b, j: (0, b, j, 0)
    bwd4 = lambda b, j: (0, b, nb - 1 - j, 0)

    def dir_specs(m3, m4):
        blk = pl.BlockSpec((grp, rb, HEADS * HEAD_DIM), m3)
        return [blk, blk, blk, blk, pl.BlockSpec((HEADS, grp, rb, CHUNK), m4),
                pl.BlockSpec((grp, cb * SUBLANES, LANES), m3)]

    out = jax.ShapeDtypeStruct((batch, seq, HEADS * HEAD_DIM), F32)
    o_f, o_b = pl.pallas_call(
        functools.partial(_gdn_scan_kernel, chunks=cb, group=grp),
        grid=(batch // grp, nb),
        in_specs=dir_specs(fwd, fwd4) + dir_specs(bwd, bwd4),
        out_specs=[pl.BlockSpec((grp, rb, HEADS * HEAD_DIM), fwd), pl.BlockSpec((grp, rb, HEADS * HEAD_DIM), bwd)],
        out_shape=[out, out],
        scratch_shapes=[pltpu.VMEM((grp * 2 * HEADS, HEAD_DIM, HEAD_DIM), F32)],
        compiler_params=_params("parallel", "arbitrary"),
        name="gdn_scan",
    )(uf, wf, qdf, kdf, af, gl, ub, wb, qdb, kdb, ab, gl)
    return o_f.reshape(batch * seq, HEADS * HEAD_DIM), o_b.reshape(batch * seq, HEADS * HEAD_DIM)


def _attn_kernel(qt_ref, k_ref, vt_ref, lq_ref, g_ref, o_ref, s_ref, p_ref, acc_ref, *, seq, kblk, lam_init):
    qb = qt_ref.shape[1]
    nblk = seq // kblk
    qt = qt_ref[...]
    row = lax.broadcasted_iota(jnp.int32, (2 * DIFF_DQK, 2 * qb), 0)
    colq = lax.broadcasted_iota(jnp.int32, (2 * DIFF_DQK, 2 * qb), 1)
    keep = (row < DIFF_DQK) == (colq < qb)
    rhs = jnp.where(keep, jnp.concatenate([qt, qt], axis=1), jnp.zeros((), BF16))
    ones = jnp.ones((ATTN_SUM_ROWS, kblk), BF16)

    def key_rows(i):
        return pl.ds(i * kblk if isinstance(i, int) else pl.multiple_of(i * kblk, kblk), kblk)

    def scores(i):
        return _dot(k_ref[key_rows(i), :], rhs)

    def weighted_values(i, slot):
        lhs = jnp.concatenate([vt_ref[:, key_rows(i)], ones], axis=0)
        return _dot(lhs, p_ref[slot])

    def step(i, r, m, first=False, ahead=True):
        if not first:
            pv = weighted_values(i - 1, (r + 1) % 2)
        if ahead:
            s_ref[(r + ATTN_AHEAD) % ATTN_S_SLOTS] = scores(i + ATTN_AHEAD)
        m_out = []
        for g in range(2 * qb // LANES):
            ls = slice(g * LANES, (g + 1) * LANES)
            s = s_ref[r % ATTN_S_SLOTS, :, ls]
            m_new = jnp.maximum(m[g], jnp.max(s, axis=0, keepdims=True))
            p_ref[r % 2, :, ls] = jnp.exp2(s - m_new).astype(BF16)
            if not first:
                acc_ref[:, ls] = (acc_ref[:, ls] + pv[:, ls]) * jnp.exp2(m[g] - m_new)
            m_out.append(m_new)
        return m_out

    for i in range(ATTN_AHEAD):
        s_ref[i] = scores(i)
    acc_ref[...] = jnp.zeros_like(acc_ref)
    m = step(0, 0, [jnp.full((1, LANES), -jnp.inf, F32)] * (2 * qb // LANES), first=True)
    for i in range(1, nblk):
        m = step(i, i, m, ahead=i + ATTN_AHEAD < nblk)
    acc = acc_ref[...] + weighted_values(nblk - 1, (nblk - 1) % 2)

    lq = lq_ref[...]
    lam = (jnp.exp(jnp.sum(lq[0:1] * lq[1:2], axis=-1, keepdims=True))
           - jnp.exp(jnp.sum(lq[2:3] * lq[3:4], axis=-1, keepdims=True)) + lam_init)
    on = acc[:HEAD_DIM] / acc[HEAD_DIM:HEAD_DIM + 1]
    o = (on[:, :qb] - lam * on[:, qb:]).T
    o = o * lax.rsqrt(jnp.mean(o * o, -1, keepdims=True) + 1e-6) * g_ref[...]
    o_ref[...] = o * (1.0 - lam_init)


def _attention(qt, k, vt, lam_qk, norm_g, batch, seq, lam_init):
    t = k.shape[0]
    qb = ATTN_Q_BLOCK
    kblk = ATTN_K_BLOCK
    nq = seq // qb
    assert ATTN_S_SLOTS % 2 == 0 and (seq // kblk) % ATTN_S_SLOTS == 0
    return pl.pallas_call(
        functools.partial(_attn_kernel, seq=seq, kblk=kblk, lam_init=lam_init),
        grid=(batch, HEADS, nq),
        in_specs=[
            pl.BlockSpec((2 * DIFF_DQK, qb), lambda b, h, i: (h, b * nq + i)),
            pl.BlockSpec((seq, 2 * DIFF_DQK), lambda b, h, i: (b, h)),
            pl.BlockSpec((HEAD_DIM, seq), lambda b, h, i: (h, b)),
            pl.BlockSpec((4, DIFF_DQK), lambda b, h, i: (0, 0)),
            pl.BlockSpec((1, HEAD_DIM), lambda b, h, i: (0, 0)),
        ],
        out_specs=pl.BlockSpec((qb, HEAD_DIM), lambda b, h, i: (b * nq + i, h)),
        out_shape=jax.ShapeDtypeStruct((t, HEADS * HEAD_DIM), F32),
        scratch_shapes=[pltpu.VMEM((ATTN_S_SLOTS, kblk, 2 * qb), F32),
                        pltpu.VMEM((2, kblk, 2 * qb), BF16),
                        pltpu.VMEM((HEAD_DIM + ATTN_SUM_ROWS, 2 * qb), F32)],
        compiler_params=_params("parallel", "parallel", "arbitrary"),
        name="diff_attn",
    )(qt, k, vt, lam_qk, norm_g)


def _outproj_kernel(of_ref, ob_ref, z_ref, oa_ref, x_ref, w_ref, gg_ref, lg_ref, lb_ref, y_ref, *, alpha):
    parts = []
    for h in range(HEADS):
        hs = slice(h * HEAD_DIM, (h + 1) * HEAD_DIM)
        o = of_ref[:, hs] + ob_ref[:, hs]
        o = o * lax.rsqrt(jnp.mean(o * o, -1, keepdims=True) + 1e-6) * gg_ref[...]
        zz = z_ref[:, hs]
        parts.append(o * (zz * _sigmoid(zz)))
    mixed = jnp.concatenate(parts + [oa_ref[...]], axis=1).astype(BF16)
    y = alpha * x_ref[...] + _dot(mixed, w_ref[...])
    y_ref[...] = _layer_norm(y, lg_ref[...], lb_ref[...])


def _outproj(o_f, o_b, z, o_attn, x, w_out, gdn_g, ln_g, ln_b, alpha):
    t = x.shape[0]
    tm = ROW_TILE
    row = lambda i: (i, 0)
    const = lambda i: (0, 0)
    half = pl.BlockSpec((tm, HEADS * HEAD_DIM), row)
    return pl.pallas_call(
        functools.partial(_outproj_kernel, alpha=alpha),
        grid=(t // tm,),
        in_specs=[half, half, half, half,
                  pl.BlockSpec((tm, D_MODEL), row),
                  pl.BlockSpec((D_MODEL, D_MODEL), const),
                  pl.BlockSpec((1, HEAD_DIM), const),
                  pl.BlockSpec((1, D_MODEL), const),
                  pl.BlockSpec((1, D_MODEL), const)],
        out_specs=pl.BlockSpec((tm, D_MODEL), row),
        out_shape=jax.ShapeDtypeStruct((t, D_MODEL), F32),
        compiler_params=_params("parallel"),
        name="outproj_ln",
    )(o_f, o_b, z, o_attn, x, w_out, gdn_g, ln_g, ln_b)


def _ffn_kernel(x_ref, wg_ref, wu_ref, wd_ref, lg_ref, lb_ref, y_ref, *, alpha):
    x = x_ref[...]
    xb = x.astype(BF16)
    gate = _dot(xb, wg_ref[...])
    up = _dot(xb, wu_ref[...])
    act = (gate * _sigmoid(gate) * up).astype(BF16)
    y = alpha * x + _dot(act, wd_ref[...])
    y_ref[...] = _layer_norm(y, lg_ref[...], lb_ref[...])


def _ffn(x, wg, wu, wd, ln_g, ln_b, alpha):
    t = x.shape[0]
    tm = ROW_TILE // 2
    row = lambda i: (i, 0)
    const = lambda i: (0, 0)
    once = pl.Buffered(1)
    return pl.pallas_call(
        functools.partial(_ffn_kernel, alpha=alpha),
        grid=(t // tm,),
        in_specs=[pl.BlockSpec((tm, D_MODEL), row),
                  pl.BlockSpec((D_MODEL, D_FF), const, pipeline_mode=once),
                  pl.BlockSpec((D_MODEL, D_FF), const, pipeline_mode=once),
                  pl.BlockSpec((D_FF, D_MODEL), const, pipeline_mode=once),
                  pl.BlockSpec((1, D_MODEL), const),
                  pl.BlockSpec((1, D_MODEL), const)],
        out_specs=pl.BlockSpec((tm, D_MODEL), row),
        out_shape=jax.ShapeDtypeStruct((t, D_MODEL), F32),
        compiler_params=_params("parallel"),
        name="swiglu_ln",
    )(x, wg, wu, wd, ln_g, ln_b)


def _rope_tables(seq):
    inv = 1.0 / (ROPE_THETA ** (jnp.arange(0, ROPE_DIM, 2, dtype=F32) / ROPE_DIM))
    ang = jnp.arange(seq, dtype=F32)[:, None] * inv[None, :]
    cos, sin = jnp.cos(ang), jnp.sin(ang)
    zero8 = jnp.zeros_like(sin)
    rest = DIFF_DQK - ROPE_DIM
    c64 = jnp.concatenate([cos, cos, jnp.ones((seq, rest), F32)], axis=1)
    s1 = jnp.concatenate([-sin, zero8, jnp.zeros((seq, rest), F32)], axis=1)
    s2 = jnp.concatenate([zero8, sin, jnp.zeros((seq, rest), F32)], axis=1)
    two = lambda a: jnp.concatenate([a, a], axis=1)
    return cos.T, sin.T, two(c64), two(s1), two(s2)


def _prep_layer(w_in, conv_w, a_log, dt_bias, gdn_norm_g, lam_qk, diff_norm_g, w_out, ln1_g, ln1_b,
                w_gate_up, w_down, ln2_g, ln2_b):
    c0 = GDN_QKV
    c1 = c0 + GDN_Z
    c2 = c1 + GDN_GATES
    c3 = c2 + DIFF_W
    c4 = c3 + DIFF_W
    w_g = w_in[:, c1:c2]
    wn = jnp.concatenate([w_in[:, :c1], w_in[:, c3:c4]], axis=1).astype(BF16)
    wt = jnp.concatenate([w_in[:, c2:c3], w_in[:, c4:], w_g], axis=1).T.astype(BF16)
    conv_w8 = jnp.pad(conv_w, ((0, SUBLANES - CONV_WIDTH), (0, 0)))
    gate_par = jnp.stack([a_log.reshape(-1), dt_bias.reshape(-1)])
    pcol = jnp.pad(gate_par.T, ((2 * HEADS, 0), (0, LANES - 2)))
    return dict(
        wn=wn, wt=wt, conv_w8=conv_w8, pcol=pcol,
        gdn_g=gdn_norm_g.reshape(1, HEAD_DIM), lam_qk=lam_qk, diff_g=diff_norm_g.reshape(1, HEAD_DIM),
        w_out=w_out.astype(BF16), ln1_g=ln1_g.reshape(1, D_MODEL), ln1_b=ln1_b.reshape(1, D_MODEL),
        wg=w_gate_up[:, :D_FF].astype(BF16), wu=w_gate_up[:, D_FF:].astype(BF16), wd=w_down.astype(BF16),
        ln2_g=ln2_g.reshape(1, D_MODEL), ln2_b=ln2_b.reshape(1, D_MODEL))


def _trunk(x, layers, alpha):
    batch, seq, _ = x.shape
    rope = _rope_tables(seq)
    xf = x.reshape(batch * seq, D_MODEL)
    for l, p in enumerate(layers):
        lam_init = 0.8 - 0.6 * math.exp(-0.3 * l)
        qkv, z, gates_t, qt, k, vt = _inproj(xf, p["wn"], p["wt"], rope, seq)
        local = _gdn_local(qkv, gates_t, p["conv_w8"], p["pcol"], seq)
        o_f, o_b = _gdn_scan(local, batch, seq)
        o_attn = _attention(qt, k, vt, p["lam_qk"], p["diff_g"], batch, seq, lam_init)
        xf = _outproj(o_f, o_b, z, o_attn, xf, p["w_out"], p["gdn_g"], p["ln1_g"], p["ln1_b"], alpha)
        xf = _ffn(xf, p["wg"], p["wu"], p["wd"], p["ln2_g"], p["ln2_b"], alpha)
    return xf.reshape(batch, seq, D_MODEL)


def kernel(x_prompt, x_sample, w_in, conv_w, a_log, dt_bias, gdn_norm_g, lam_qk, diff_norm_g, w_out,
           ln1_g, ln1_b, w_gate_up, w_down, ln2_g, ln2_b):
    depth = w_in.shape[0]
    alpha = (2 * depth) ** 0.25
    layers = [_prep_layer(w_in[l], conv_w[l], a_log[l], dt_bias[l], gdn_norm_g[l], lam_qk[l], diff_norm_g[l],
                          w_out[l], ln1_g[l], ln1_b[l], w_gate_up[l], w_down[l], ln2_g[l], ln2_b[l])
              for l in range(depth)]
    return (_trunk(x_prompt, layers, alpha), _trunk(x_sample, layers, alpha))
```

```python
import functools
import math

import jax
import jax.numpy as jnp
from jax import lax
from jax.experimental import pallas as pl
from jax.experimental.pallas import tpu as pltpu

F32 = jnp.float32
BF16 = jnp.bfloat16

D_MODEL = 1024
HEADS = 4
HEAD_DIM = 128
GDN_QKV = 3 * HEADS * HEAD_DIM
GDN_Z = HEADS * HEAD_DIM
GDN_GATES = 4 * HEADS
DIFF_DQK = 64
DIFF_W = HEADS * 2 * DIFF_DQK
CONV_WIDTH = 5
CONV_PAD = CONV_WIDTH // 2
CHUNK = 64
ROPE_THETA = 500000.0
ROPE_DIM = DIFF_DQK // 4
ROPE_HALF = ROPE_DIM // 2
D_FF = int(math.ceil(8 * D_MODEL / 3 / 256)) * 256

LANES = 128
SUBLANES = 8
VMEM_LIMIT = 56 * 1024 * 1024

ROW_TILE = 512
GDN_LOCAL_ROWS = 256
GDN_SCAN_CHUNKS = 4
GDN_SCAN_GROUP = 4
ATTN_Q_BLOCK = 512
ATTN_AHEAD = 2
ATTN_S_SLOTS = 4
ATTN_K_BLOCK = 512
ATTN_SUM_ROWS = 16
LOG2E = 1.4426950408889634


def _params(*sem):
    return pltpu.CompilerParams(dimension_semantics=sem, vmem_limit_bytes=VMEM_LIMIT)


def _sigmoid(x):
    return 1.0 / (1.0 + jnp.exp(-x))


def _softplus(x):
    return jnp.maximum(x, 0.0) + jnp.log(1.0 + jnp.exp(-jnp.abs(x)))


def _dot(a, b):
    return jnp.dot(a, b, preferred_element_type=F32)


def _dot_nt(a, b):
    return lax.dot_general(a, b, (((1,), (1,)), ((), ())), preferred_element_type=F32)


def _dot_tn(a, b):
    return lax.dot_general(a, b, (((0,), (0,)), ((), ())), preferred_element_type=F32)


def _layer_norm(y, g, b):
    mu = jnp.mean(y, -1, keepdims=True)
    d = y - mu
    var = jnp.mean(d * d, -1, keepdims=True)
    return d * lax.rsqrt(var + 1e-5) * g + b


def _inproj_kernel(x_ref, wn_ref, wt_ref, cos_ref, sin_ref, kc_ref, ks1_ref, ks2_ref,
                   qkv_ref, z_ref, gt_ref, qt_ref, k_ref, vt_ref):
    xb = x_ref[...].astype(BF16)
    hn = _dot(xb, wn_ref[...])
    qkv_ref[...] = hn[:, :GDN_QKV]
    z_ref[...] = hn[:, GDN_QKV:GDN_QKV + GDN_Z]
    dk0 = GDN_QKV + GDN_Z
    kc, ks1, ks2 = kc_ref[...], ks1_ref[...], ks2_ref[...]
    for h in range(HEADS):
        kh = hn[:, dk0 + h * LANES:dk0 + (h + 1) * LANES]
        up = pltpu.roll(kh, LANES - ROPE_HALF, axis=1)
        dn = pltpu.roll(kh, ROPE_HALF, axis=1)
        k_ref[:, h * LANES:(h + 1) * LANES] = (kh * kc + up * ks1 + dn * ks2).astype(BF16)
    ht = _dot_nt(wt_ref[...], xb)
    vt_ref[...] = ht[DIFF_W:2 * DIFF_W].astype(BF16)
    gt_ref[...] = ht[2 * DIFF_W:2 * DIFF_W + GDN_GATES]
    cos, sin = cos_ref[...], sin_ref[...]
    pieces = []
    for c in range(DIFF_W // DIFF_DQK):
        r = c * DIFF_DQK
        a, b = ht[r:r + ROPE_HALF], ht[r + ROPE_HALF:r + ROPE_DIM]
        pieces += [a * cos - b * sin, b * cos + a * sin, ht[r + ROPE_DIM:r + DIFF_DQK]]
    qt_ref[...] = (jnp.concatenate(pieces, axis=0) * (DIFF_DQK ** -0.5 * LOG2E)).astype(BF16)


def _inproj(x, wn, wt, rope, seq):
    t = x.shape[0]
    tm = ROW_TILE
    nps = seq // tm
    cos_t, sin_t, kc, ks1, ks2 = rope
    row = lambda i: (i, 0)
    col = lambda i: (0, i)
    const = lambda i: (0, 0)
    pos_row = lambda i: (i % nps, 0)
    pos_col = lambda i: (0, i % nps)
    n_cols = wn.shape[1]
    t_rows = wt.shape[0]
    return pl.pallas_call(
        _inproj_kernel,
        grid=(t // tm,),
        in_specs=[
            pl.BlockSpec((tm, D_MODEL), row),
            pl.BlockSpec((D_MODEL, n_cols), const),
            pl.BlockSpec((t_rows, D_MODEL), const),
            pl.BlockSpec((ROPE_HALF, tm), pos_col),
            pl.BlockSpec((ROPE_HALF, tm), pos_col),
            pl.BlockSpec((tm, LANES), pos_row),
            pl.BlockSpec((tm, LANES), pos_row),
            pl.BlockSpec((tm, LANES), pos_row),
        ],
        out_specs=[
            pl.BlockSpec((tm, GDN_QKV), row),
            pl.BlockSpec((tm, GDN_Z), row),
            pl.BlockSpec((GDN_GATES, tm), col),
            pl.BlockSpec((DIFF_W, tm), col),
            pl.BlockSpec((tm, DIFF_W), row),
            pl.BlockSpec((DIFF_W, tm), col),
        ],
        out_shape=[
            jax.ShapeDtypeStruct((t, GDN_QKV), F32),
            jax.ShapeDtypeStruct((t, GDN_Z), F32),
            jax.ShapeDtypeStruct((GDN_GATES, t), F32),
            jax.ShapeDtypeStruct((DIFF_W, t), BF16),
            jax.ShapeDtypeStruct((t, DIFF_W), BF16),
            jax.ShapeDtypeStruct((DIFF_W, t), BF16),
        ],
        compiler_params=_params("parallel"),
        name="inproj",
    )(x, wn, wt, cos_t, sin_t, kc, ks1, ks2)


def _unit_tri_inverses(ms, eye, same16, mid32, same32):
    b16 = lambda a: a.astype(BF16)
    ds = [b16(jnp.where(same16, m, 0.0)) for m in ms]
    c32s = [b16(jnp.where(mid32, m, 0.0)) for m in ms]
    c64s = [b16(jnp.where(same32, 0.0, m)) for m in ms]
    d2s = [b16(_dot(d, d)) for d in ds]
    d4s = [b16(_dot(d2, d2)) for d2 in d2s]
    d8s = [b16(_dot(d4, d4)) for d4 in d4s]
    ts = [eye - d for d in ds]
    for powers in (d2s, d4s, d8s):
        ts = [t + _dot(b16(t), p) for t, p in zip(ts, powers)]
    for corners in (c32s, c64s):
        tbs = [b16(t) for t in ts]
        inner = [b16(_dot(c, tb)) for c, tb in zip(corners, tbs)]
        ts = [t - _dot(tb, i) for t, tb, i in zip(ts, tbs, inner)]
    return ts


def _gdn_local_kernel(qkv_ref, prev_ref, next_ref, gt_ref, cw_ref, pcol_ref,
                      uf_ref, wf_ref, qdf_ref, kdf_ref, af_ref,
                      ub_ref, wb_ref, qdb_ref, kdb_ref, ab_ref, gl_ref, *, blocks_per_seq):
    sb = qkv_ref.shape[0]
    j = pl.program_id(0) % blocks_per_seq
    prev_on = j > 0
    next_on = j < blocks_per_seq - 1
    ext_rows = sb + 2 * SUBLANES

    conv = []
    for cg in range(GDN_QKV // LANES):
        sl = slice(cg * LANES, (cg + 1) * LANES)
        ext = jnp.concatenate([jnp.where(prev_on, prev_ref[:, sl], 0.0), qkv_ref[:, sl],
                               jnp.where(next_on, next_ref[:, sl], 0.0)], axis=0)
        acc = None
        for tap in range(CONV_WIDTH):
            shift = (CONV_PAD - tap) % ext_rows
            rolled = ext if shift == 0 else pltpu.roll(ext, shift, axis=0)
            term = rolled[SUBLANES:SUBLANES + sb] * cw_ref[tap:tap + 1, sl]
            acc = term if acc is None else acc + term
        conv.append(acc * _sigmoid(acc))

    def l2n(t):
        return t * lax.rsqrt(jnp.sum(t * t, -1, keepdims=True) + 1e-6)

    nch = 2 * HEADS
    g_r = gt_ref[...]
    beta_r = _sigmoid(g_r[:nch])
    gval = -jnp.exp(pcol_ref[nch:, 0:1]) * _softplus(g_r[nch:] + pcol_ref[nch:, 1:2])

    pos = lax.broadcasted_iota(jnp.int32, (nch, sb), 1) % CHUNK
    pre, suf = gval, gval
    step = 1
    while step < CHUNK:
        pre = pre + jnp.where(pos >= step, pltpu.roll(pre, step, axis=1), 0.0)
        suf = suf + jnp.where(pos < CHUNK - step, pltpu.roll(suf, sb - step, axis=1), 0.0)
        step *= 2
    gtot = pre + suf - gval
    is_fwd = lax.broadcasted_iota(jnp.int32, (nch, sb), 0) < HEADS
    gc_r = jnp.where(is_fwd, pre, suf)
    for c in range(sb // CHUNK):
        gl = jnp.exp(gtot[:, c * CHUNK:c * CHUNK + 1])
        gl_ref[c * SUBLANES:(c + 1) * SUBLANES, :] = jnp.broadcast_to(gl, (SUBLANES, LANES))
    stack = jnp.concatenate([beta_r, gc_r, jnp.exp(gc_r), jnp.exp(gtot - gc_r),
                             jnp.zeros((LANES - 4 * nch, sb), F32)], axis=0)
    cols = stack.T

    ri = lax.broadcasted_iota(jnp.int32, (sb, sb), 0)
    ci = lax.broadcasted_iota(jnp.int32, (sb, sb), 1)
    same64 = (ri // CHUNK) == (ci // CHUNK)
    same32 = (ri // 32) == (ci // 32)
    same16 = (ri // 16) == (ci // 16)
    mid32 = jnp.logical_and(same32, jnp.logical_not(same16))
    diag = ri == ci
    eye = jnp.where(diag, 1.0, 0.0).astype(F32)
    incl = (jnp.logical_and(same64, ci <= ri), jnp.logical_and(same64, ci >= ri))

    u_refs, w_refs, qd_refs, kd_refs, a_refs = ((uf_ref, ub_ref), (wf_ref, wb_ref), (qdf_ref, qdb_ref),
                                                (kdf_ref, kdb_ref), (af_ref, ab_ref))
    qs = [l2n(conv[h]) * (HEAD_DIM ** -0.5) for h in range(HEADS)]
    ks = [l2n(conv[HEADS + h]) for h in range(HEADS)]
    vs = [conv[2 * HEADS + h] for h in range(HEADS)]
    kbs = [k.astype(BF16) for k in ks]
    kks = [_dot_nt(kb, kb) for kb in kbs]
    qks = [_dot_nt(q.astype(BF16), kb) for q, kb in zip(qs, kbs)]

    chains = [(h, d) for h in range(HEADS) for d in range(2)]
    col = lambda group, h, d: cols[:, group * nch + d * HEADS + h:group * nch + d * HEADS + h + 1]
    ms = []
    for h, d in chains:
        r = d * HEADS + h
        decay = jnp.exp(jnp.where(incl[d], col(1, h, d) - gc_r[r:r + 1, :], -jnp.inf))
        ms.append(jnp.where(diag, 0.0, kks[h] * col(0, h, d) * decay))
        attn = (qks[h] * decay).astype(BF16)
        for c in range(sb // CHUNK):
            rows = slice(c * CHUNK, (c + 1) * CHUNK)
            a_refs[d][h, rows, :] = attn[rows, rows]
    tinvs = _unit_tri_inverses(ms, eye, same16, mid32, same32)
    rhss = [jnp.concatenate([vs[h] * col(0, h, d), ks[h] * (col(0, h, d) * col(2, h, d))], axis=1).astype(BF16)
            for h, d in chains]
    sols = [_dot(t.astype(BF16), rhs) for t, rhs in zip(tinvs, rhss)]
    for (h, d), sol in zip(chains, sols):
        hs = slice(h * HEAD_DIM, (h + 1) * HEAD_DIM)
        u_refs[d][:, hs] = sol[:, :HEAD_DIM]
        w_refs[d][:, hs] = sol[:, HEAD_DIM:].astype(BF16)
        qd_refs[d][:, hs] = (qs[h] * col(2, h, d)).astype(BF16)
        kd_refs[d][:, hs] = (ks[h] * col(3, h, d)).astype(BF16)


def _gdn_local(qkv, gates_t, conv_w8, pcol, seq):
    t = qkv.shape[0]
    sb = GDN_LOCAL_ROWS
    bps = seq // sb
    hb = sb // SUBLANES
    n8 = t // SUBLANES
    row = lambda i: (i, 0)
    const = lambda i: (0, 0)
    wide = jax.ShapeDtypeStruct((t, HEADS * HEAD_DIM), F32)
    wide_bf = jax.ShapeDtypeStruct((t, HEADS * HEAD_DIM), BF16)
    attn = jax.ShapeDtypeStruct((HEADS, t, CHUNK), BF16)
    wide_spec = pl.BlockSpec((sb, HEADS * HEAD_DIM), row)
    attn_spec = pl.BlockSpec((HEADS, sb, CHUNK), lambda i: (0, i, 0))
    dir_shapes = [wide, wide_bf, wide_bf, wide_bf, attn]
    dir_specs = [wide_spec, wide_spec, wide_spec, wide_spec, attn_spec]
    return pl.pallas_call(
        functools.partial(_gdn_local_kernel, blocks_per_seq=bps),
        grid=(t // sb,),
        in_specs=[
            pl.BlockSpec((sb, GDN_QKV), row),
            pl.BlockSpec((SUBLANES, GDN_QKV), lambda i: (jnp.maximum(i * hb - 1, 0), 0)),
            pl.BlockSpec((SUBLANES, GDN_QKV), lambda i: (jnp.minimum((i + 1) * hb, n8 - 1), 0)),
            pl.BlockSpec((GDN_GATES, sb), lambda i: (0, i)),
            pl.BlockSpec((SUBLANES, GDN_QKV), const),
            pl.BlockSpec((GDN_GATES, LANES), const),
        ],
        out_specs=dir_specs + dir_specs + [pl.BlockSpec((sb // CHUNK * SUBLANES, LANES), row)],
        out_shape=dir_shapes + dir_shapes + [jax.ShapeDtypeStruct((t // CHUNK * SUBLANES, LANES), F32)],
        compiler_params=_params("parallel"),
        name="gdn_local",
    )(qkv, qkv, qkv, gates_t, conv_w8, pcol)


def _gdn_scan_kernel(uf_ref, wf_ref, qdf_ref, kdf_ref, af_ref, glf_ref,
                     ub_ref, wb_ref, qdb_ref, kdb_ref, ab_ref, glb_ref,
                     of_ref, ob_ref, state_ref, *, chunks, group):
    @pl.when(pl.program_id(1) == 0)
    def _():
        state_ref[...] = jnp.zeros_like(state_ref)

    dirs = ((uf_ref, wf_ref, qdf_ref, kdf_ref, af_ref, glf_ref, of_ref),
            (ub_ref, wb_ref, qdb_ref, kdb_ref, ab_ref, glb_ref, ob_ref))
    chains = [(g, d, h) for g in range(group) for d in range(2) for h in range(HEADS)]

    def body(c, carry):
        chunk = (c, chunks - 1 - c)
        rows = tuple(pl.ds(pl.multiple_of(ch * CHUNK, CHUNK), CHUNK) for ch in chunk)
        hs = lambda h: slice(h * HEAD_DIM, (h + 1) * HEAD_DIM)
        slot = lambda g, d, h: (g * 2 + d) * HEADS + h
        ws_qs = []
        for g, d, h in chains:
            w_ref, qd_ref = dirs[d][1], dirs[d][2]
            lhs = jnp.concatenate([w_ref[g, rows[d], hs(h)], qd_ref[g, rows[d], hs(h)]], axis=0)
            ws_qs.append(_dot(lhs, state_ref[slot(g, d, h)].astype(BF16)))
        vbs = [(dirs[d][0][g, rows[d], hs(h)] - wq[:CHUNK]).astype(BF16) for (g, d, h), wq in zip(chains, ws_qs)]
        for (g, d, h), wq, vb in zip(chains, ws_qs, vbs):
            kd_ref, a_ref, gl_ref, o_ref = dirs[d][3], dirs[d][4], dirs[d][5], dirs[d][6]
            o_ref[g, rows[d], hs(h)] = wq[CHUNK:] + _dot(a_ref[h, g, rows[d], :], vb)
            gl = gl_ref[g, pl.ds(chunk[d] * SUBLANES + d * HEADS + h, 1), :]
            state_ref[slot(g, d, h)] = state_ref[slot(g, d, h)] * gl + _dot_tn(kd_ref[g, rows[d], hs(h)], vb)
        return carry

    lax.fori_loop(0, chunks, body, 0)


def _gdn_scan(local_out, batch, seq):
    cb = GDN_SCAN_CHUNKS
    grp = GDN_SCAN_GROUP
    rb = cb * CHUNK
    nb = seq // rb
    wide = lambda a: a.reshape(batch, seq, HEADS * HEAD_DIM)
    uf, wf, qdf, kdf, af, ub, wb, qdb, kdb, ab, gl = local_out
    uf, wf, qdf, kdf, ub, wb, qdb, kdb = (wide(a) for a in (uf, wf, qdf, kdf, ub, wb, qdb, kdb))
    af, ab = (a.reshape(HEADS, batch, seq, CHUNK) for a in (af, ab))
    gl = gl.reshape(batch, seq // CHUNK * SUBLANES, LANES)
    fwd = lambda b, j: (b, j, 0)
    bwd = lambda b, j: (b, nb - 1 - j, 0)
    fwd4 = lambda b, j: (0, b, j, 0)
    bwd4 = lambda b, j: (0, b, nb - 1 - j, 0)

    def dir_specs(m3, m4):
        blk = pl.BlockSpec((grp, rb, HEADS * HEAD_DIM), m3)
        return [blk, blk, blk, blk, pl.BlockSpec((HEADS, grp, rb, CHUNK), m4),
                pl.BlockSpec((grp, cb * SUBLANES, LANES), m3)]

    out = jax.ShapeDtypeStruct((batch, seq, HEADS * HEAD_DIM), F32)
    o_f, o_b = pl.pallas_call(
        functools.partial(_gdn_scan_kernel, chunks=cb, group=grp),
        grid=(batch // grp, nb),
        in_specs=dir_specs(fwd, fwd4) + dir_specs(bwd, bwd4),
        out_specs=[pl.BlockSpec((grp, rb, HEADS * HEAD_DIM), fwd), pl.BlockSpec((grp, rb, HEADS * HEAD_DIM), bwd)],
        out_shape=[out, out],
        scratch_shapes=[pltpu.VMEM((grp * 2 * HEADS, HEAD_DIM, HEAD_DIM), F32)],
        compiler_params=_params("parallel", "arbitrary"),
        name="gdn_scan",
    )(uf, wf, qdf, kdf, af, gl, ub, wb, qdb, kdb, ab, gl)
    return o_f.reshape(batch * seq, HEADS * HEAD_DIM), o_b.reshape(batch * seq, HEADS * HEAD_DIM)


def _attn_kernel(qt_ref, k_ref, vt_ref, lq_ref, g_ref, o_ref, s_ref, p_ref, acc_ref, *, seq, kblk, lam_init):
    qb = qt_ref.shape[1]
    nblk = seq // kblk
    qt = qt_ref[...]
    row = lax.broadcasted_iota(jnp.int32, (2 * DIFF_DQK, 2 * qb), 0)
    colq = lax.broadcasted_iota(jnp.int32, (2 * DIFF_DQK, 2 * qb), 1)
    keep = (row < DIFF_DQK) == (colq < qb)
    rhs = jnp.where(keep, jnp.concatenate([qt, qt], axis=1), jnp.zeros((), BF16))
    ones = jnp.ones((ATTN_SUM_ROWS, kblk), BF16)

    def key_rows(i):
        return pl.ds(i * kblk if isinstance(i, int) else pl.multiple_of(i * kblk, kblk), kblk)

    groups = [(g, slice(g * LANES, (g + 1) * LANES)) for g in range(2 * qb // LANES)]

    def score_block(i, slot):
        sc = _dot(k_ref[key_rows(i), :], rhs)
        for g, ls in groups:
            s_ref[slot, g] = sc[:, ls]

    def weighted_values(i, slot):
        lhs = jnp.concatenate([vt_ref[:, key_rows(i)], ones], axis=0)
        return _dot(lhs, jnp.concatenate([p_ref[slot, g] for g, _ in groups], axis=1))

    def step(i, r, m, first=False, ahead=True):
        if not first:
            pv = weighted_values(i - 1, (r + 1) % 2)
        if ahead:
            score_block(i + ATTN_AHEAD, (r + ATTN_AHEAD) % ATTN_S_SLOTS)
        m_out = []
        for g, ls in groups:
            s = s_ref[r % ATTN_S_SLOTS, g]
            m_new = jnp.maximum(m[g], jnp.max(s, axis=0, keepdims=True))
            p_ref[r % 2, g] = jnp.exp2(s - m_new).astype(BF16)
            if not first:
                acc_ref[g] = (acc_ref[g] + pv[:, ls]) * jnp.exp2(m[g] - m_new)
            m_out.append(m_new)
        return m_out

    for i in range(ATTN_AHEAD):
        score_block(i, i)
    acc_ref[...] = jnp.zeros_like(acc_ref)
    m = step(0, 0, [jnp.full((1, LANES), -jnp.inf, F32)] * len(groups), first=True)
    for i in range(1, nblk):
        m = step(i, i, m, ahead=i + ATTN_AHEAD < nblk)
    pv = weighted_values(nblk - 1, (nblk - 1) % 2)
    acc = jnp.concatenate([acc_ref[g] + pv[:, ls] for g, ls in groups], axis=1)

    lq = lq_ref[...]
    lam = (jnp.exp(jnp.sum(lq[0:1] * lq[1:2], axis=-1, keepdims=True))
           - jnp.exp(jnp.sum(lq[2:3] * lq[3:4], axis=-1, keepdims=True)) + lam_init)
    on = acc[:HEAD_DIM] / acc[HEAD_DIM:HEAD_DIM + 1]
    o = (on[:, :qb] - lam * on[:, qb:]).T
    o = o * lax.rsqrt(jnp.mean(o * o, -1, keepdims=True) + 1e-6) * g_ref[...]
    o_ref[...] = (o * (1.0 - lam_init)).astype(o_ref.dtype)


def _attention(qt, k, vt, lam_qk, norm_g, batch, seq, lam_init):
    t = k.shape[0]
    qb = ATTN_Q_BLOCK
    kblk = ATTN_K_BLOCK
    nq = seq // qb
    assert ATTN_S_SLOTS % 2 == 0 and (seq // kblk) % ATTN_S_SLOTS == 0
    return pl.pallas_call(
        functools.partial(_attn_kernel, seq=seq, kblk=kblk, lam_init=lam_init),
        grid=(batch, HEADS, nq),
        in_specs=[
            pl.BlockSpec((2 * DIFF_DQK, qb), lambda b, h, i: (h, b * nq + i)),
            pl.BlockSpec((seq, 2 * DIFF_DQK), lambda b, h, i: (b, h)),
            pl.BlockSpec((HEAD_DIM, seq), lambda b, h, i: (h, b)),
            pl.BlockSpec((4, DIFF_DQK), lambda b, h, i: (0, 0)),
            pl.BlockSpec((1, HEAD_DIM), lambda b, h, i: (0, 0)),
        ],
        out_specs=pl.BlockSpec((qb, HEAD_DIM), lambda b, h, i: (b * nq + i, h)),
        out_shape=jax.ShapeDtypeStruct((t, HEADS * HEAD_DIM), BF16),
        scratch_shapes=[pltpu.VMEM((ATTN_S_SLOTS, 2 * qb // LANES, kblk, LANES), F32),
                        pltpu.VMEM((2, 2 * qb // LANES, kblk, LANES), BF16),
                        pltpu.VMEM((2 * qb // LANES, HEAD_DIM + ATTN_SUM_ROWS, LANES), F32)],
        compiler_params=_params("parallel", "parallel", "arbitrary"),
        name="diff_attn",
    )(qt, k, vt, lam_qk, norm_g)


def _mix_ffn_kernel(of_ref, ob_ref, z_ref, oa_ref, x_ref, wo_ref, gg_ref, l1g_ref, l1b_ref,
                    wg_ref, wu_ref, wd_ref, l2g_ref, l2b_ref, y_ref, *, alpha):
    parts = []
    for h in range(HEADS):
        hs = slice(h * HEAD_DIM, (h + 1) * HEAD_DIM)
        o = of_ref[:, hs] + ob_ref[:, hs]
        o = o * lax.rsqrt(jnp.mean(o * o, -1, keepdims=True) + 1e-6) * gg_ref[...]
        zz = z_ref[:, hs]
        parts.append((o * (zz * _sigmoid(zz))).astype(BF16))
    mixed = jnp.concatenate(parts + [oa_ref[...]], axis=1)
    x1 = _layer_norm(alpha * x_ref[...] + _dot(mixed, wo_ref[...]), l1g_ref[...], l1b_ref[...])
    xb = x1.astype(BF16)
    gate = _dot(xb, wg_ref[...])
    up = _dot(xb, wu_ref[...])
    act = (gate * _sigmoid(gate) * up).astype(BF16)
    y = alpha * x1 + _dot(act, wd_ref[...])
    y_ref[...] = _layer_norm(y, l2g_ref[...], l2b_ref[...])


def _mix_ffn(o_f, o_b, z, o_attn, x, p, alpha):
    t = x.shape[0]
    tm = ROW_TILE // 2
    row = lambda i: (i, 0)
    const = lambda i: (0, 0)
    once = pl.Buffered(1)
    half = pl.BlockSpec((tm, HEADS * HEAD_DIM), row)
    vec = pl.BlockSpec((1, D_MODEL), const)
    return pl.pallas_call(
        functools.partial(_mix_ffn_kernel, alpha=alpha),
        grid=(t // tm,),
        in_specs=[half, half, half, half,
                  pl.BlockSpec((tm, D_MODEL), row),
                  pl.BlockSpec((D_MODEL, D_MODEL), const, pipeline_mode=once),
                  pl.BlockSpec((1, HEAD_DIM), const), vec, vec,
                  pl.BlockSpec((D_MODEL, D_FF), const, pipeline_mode=once),
                  pl.BlockSpec((D_MODEL, D_FF), const, pipeline_mode=once),
                  pl.BlockSpec((D_FF, D_MODEL), const, pipeline_mode=once),
                  vec, vec],
        out_specs=pl.BlockSpec((tm, D_MODEL), row),
        out_shape=jax.ShapeDtypeStruct((t, D_MODEL), F32),
        compiler_params=_params("parallel"),
        name="mix_ffn",
    )(o_f, o_b, z, o_attn, x, p["w_out"], p["gdn_g"], p["ln1_g"], p["ln1_b"],
      p["wg"], p["wu"], p["wd"], p["ln2_g"], p["ln2_b"])


def _rope_tables(seq):
    inv = 1.0 / (ROPE_THETA ** (jnp.arange(0, ROPE_DIM, 2, dtype=F32) / ROPE_DIM))
    ang = jnp.arange(seq, dtype=F32)[:, None] * inv[None, :]
    cos, sin = jnp.cos(ang), jnp.sin(ang)
    zero8 = jnp.zeros_like(sin)
    rest = DIFF_DQK - ROPE_DIM
    c64 = jnp.concatenate([cos, cos, jnp.ones((seq, rest), F32)], axis=1)
    s1 = jnp.concatenate([-sin, zero8, jnp.zeros((seq, rest), F32)], axis=1)
    s2 = jnp.concatenate([zero8, sin, jnp.zeros((seq, rest), F32)], axis=1)
    two = lambda a: jnp.concatenate([a, a], axis=1)
    return cos.T, sin.T, two(c64), two(s1), two(s2)


def _prep_layer(w_in, conv_w, a_log, dt_bias, gdn_norm_g, lam_qk, diff_norm_g, w_out, ln1_g, ln1_b,
                w_gate_up, w_down, ln2_g, ln2_b):
    c0 = GDN_QKV
    c1 = c0 + GDN_Z
    c2 = c1 + GDN_GATES
    c3 = c2 + DIFF_W
    c4 = c3 + DIFF_W
    w_g = w_in[:, c1:c2]
    wn = jnp.concatenate([w_in[:, :c1], w_in[:, c3:c4]], axis=1).astype(BF16)
    wt = jnp.concatenate([w_in[:, c2:c3], w_in[:, c4:], w_g], axis=1).T.astype(BF16)
    conv_w8 = jnp.pad(conv_w, ((0, SUBLANES - CONV_WIDTH), (0, 0)))
    gate_par = jnp.stack([a_log.reshape(-1), dt_bias.reshape(-1)])
    pcol = jnp.pad(gate_par.T, ((2 * HEADS, 0), (0, LANES - 2)))
    return dict(
        wn=wn, wt=wt, conv_w8=conv_w8, pcol=pcol,
        gdn_g=gdn_norm_g.reshape(1, HEAD_DIM), lam_qk=lam_qk, diff_g=diff_norm_g.reshape(1, HEAD_DIM),
        w_out=w_out.astype(BF16), ln1_g=ln1_g.reshape(1, D_MODEL), ln1_b=ln1_b.reshape(1, D_MODEL),
        wg=w_gate_up[:, :D_FF].astype(BF16), wu=w_gate_up[:, D_FF:].astype(BF16), wd=w_down.astype(BF16),
        ln2_g=ln2_g.reshape(1, D_MODEL), ln2_b=ln2_b.reshape(1, D_MODEL))


def _trunk(x, layers, alpha):
    batch, seq, _ = x.shape
    rope = _rope_tables(seq)
    xf = x.reshape(batch * seq, D_MODEL)
    for l, p in enumerate(layers):
        lam_init = 0.8 - 0.6 * math.exp(-0.3 * l)
        qkv, z, gates_t, qt, k, vt = _inproj(xf, p["wn"], p["wt"], rope, seq)
        local = _gdn_local(qkv, gates_t, p["conv_w8"], p["pcol"], seq)
        o_f, o_b = _gdn_scan(local, batch, seq)
        o_attn = _attention(qt, k, vt, p["lam_qk"], p["diff_g"], batch, seq, lam_init)
        xf = _mix_ffn(o_f, o_b, z, o_attn, xf, p, alpha)
    return xf.reshape(batch, seq, D_MODEL)


def kernel(x_prompt, x_sample, w_in, conv_w, a_log, dt_bias, gdn_norm_g, lam_qk, diff_norm_g, w_out,
           ln1_g, ln1_b, w_gate_up, w_down, ln2_g, ln2_b):
    depth = w_in.shape[0]
    alpha = (2 * depth) ** 0.25
    layers = [_prep_layer(w_in[l], conv_w[l], a_log[l], dt_bias[l], gdn_norm_g[l], lam_qk[l], diff_norm_g[l],
                          w_out[l], ln1_g[l], ln1_b[l], w_gate_up[l], w_down[l], ln2_g[l], ln2_b[l])
              for l in range(depth)]
    return (_trunk(x_prompt, layers, alpha), _trunk(x_sample, layers, alpha))
```

```python
import functools
import math

import jax
import jax.numpy as jnp
from jax import lax
from jax.experimental import pallas as pl
from jax.experimental.pallas import tpu as pltpu

F32 = jnp.float32
BF16 = jnp.bfloat16

D_MODEL = 1024
HEADS = 4
HEAD_DIM = 128
GDN_QKV = 3 * HEADS * HEAD_DIM
GDN_Z = HEADS * HEAD_DIM
GDN_GATES = 4 * HEADS
DIFF_DQK = 64
DIFF_W = HEADS * 2 * DIFF_DQK
CONV_WIDTH = 5
CONV_PAD = CONV_WIDTH // 2
CHUNK = 64
ROPE_THETA = 500000.0
ROPE_DIM = DIFF_DQK // 4
ROPE_HALF = ROPE_DIM // 2
D_FF = int(math.ceil(8 * D_MODEL / 3 / 256)) * 256

LANES = 128
SUBLANES = 8
VMEM_LIMIT = 56 * 1024 * 1024

ROW_TILE = 512
MIX_FFN_PARTS = 2
GDN_LOCAL_ROWS = 512
GDN_MATRIX_ROWS = 256
GDN_SCAN_CHUNKS = 4
GDN_SCAN_GROUP = 4
ATTN_Q_BLOCK = 512
ATTN_CHAINS = 2
ATTN_AHEAD = 2
ATTN_S_SLOTS = 4
ATTN_K_BLOCK = 512
ATTN_SUM_ROWS = 16
LOG2E = 1.4426950408889634


def _params(*sem):
    return pltpu.CompilerParams(dimension_semantics=sem, vmem_limit_bytes=VMEM_LIMIT)


def _sigmoid(x):
    return 1.0 / (1.0 + jnp.exp(-x))


def _softplus(x):
    return jnp.maximum(x, 0.0) + jnp.log(1.0 + jnp.exp(-jnp.abs(x)))


def _dot(a, b):
    return jnp.dot(a, b, preferred_element_type=F32)


def _dot_nt(a, b):
    return lax.dot_general(a, b, (((1,), (1,)), ((), ())), preferred_element_type=F32)


def _dot_tn(a, b):
    return lax.dot_general(a, b, (((0,), (0,)), ((), ())), preferred_element_type=F32)


def _layer_norm(y, g, b):
    mu = jnp.mean(y, -1, keepdims=True)
    d = y - mu
    var = jnp.mean(d * d, -1, keepdims=True)
    return d * lax.rsqrt(var + 1e-5) * g + b


def _inproj_kernel(x_ref, wn_ref, wt_ref, cos_ref, sin_ref, kc_ref, ks1_ref, ks2_ref,
                   qkv_ref, z_ref, gt_ref, qt_ref, k_ref, vt_ref):
    xb = x_ref[...].astype(BF16)
    hn = _dot(xb, wn_ref[...])
    qkv_ref[...] = hn[:, :GDN_QKV]
    z_ref[...] = hn[:, GDN_QKV:GDN_QKV + GDN_Z]
    dk0 = GDN_QKV + GDN_Z
    kc, ks1, ks2 = kc_ref[...], ks1_ref[...], ks2_ref[...]
    for h in range(HEADS):
        kh = hn[:, dk0 + h * LANES:dk0 + (h + 1) * LANES]
        up = pltpu.roll(kh, LANES - ROPE_HALF, axis=1)
        dn = pltpu.roll(kh, ROPE_HALF, axis=1)
        k_ref[:, h * LANES:(h + 1) * LANES] = (kh * kc + up * ks1 + dn * ks2).astype(BF16)
    ht = _dot_nt(wt_ref[...], xb)
    vt_ref[...] = ht[DIFF_W:2 * DIFF_W].astype(BF16)
    gt_ref[...] = ht[2 * DIFF_W:2 * DIFF_W + GDN_GATES]
    cos, sin = cos_ref[...], sin_ref[...]
    pieces = []
    for c in range(DIFF_W // DIFF_DQK):
        r = c * DIFF_DQK
        a, b = ht[r:r + ROPE_HALF], ht[r + ROPE_HALF:r + ROPE_DIM]
        pieces += [a * cos - b * sin, b * cos + a * sin, ht[r + ROPE_DIM:r + DIFF_DQK]]
    qt_ref[...] = (jnp.concatenate(pieces, axis=0) * (DIFF_DQK ** -0.5 * LOG2E)).astype(BF16)


def _inproj(x, wn, wt, rope, seq):
    t = x.shape[0]
    tm = ROW_TILE
    nps = seq // tm
    cos_t, sin_t, kc, ks1, ks2 = rope
    row = lambda i: (i, 0)
    col = lambda i: (0, i)
    const = lambda i: (0, 0)
    pos_row = lambda i: (i % nps, 0)
    pos_col = lambda i: (0, i % nps)
    n_cols = wn.shape[1]
    t_rows = wt.shape[0]
    return pl.pallas_call(
        _inproj_kernel,
        grid=(t // tm,),
        in_specs=[
            pl.BlockSpec((tm, D_MODEL), row),
            pl.BlockSpec((D_MODEL, n_cols), const),
            pl.BlockSpec((t_rows, D_MODEL), const),
            pl.BlockSpec((ROPE_HALF, tm), pos_col),
            pl.BlockSpec((ROPE_HALF, tm), pos_col),
            pl.BlockSpec((tm, LANES), pos_row),
            pl.BlockSpec((tm, LANES), pos_row),
            pl.BlockSpec((tm, LANES), pos_row),
        ],
        out_specs=[
            pl.BlockSpec((tm, GDN_QKV), row),
            pl.BlockSpec((tm, GDN_Z), row),
            pl.BlockSpec((GDN_GATES, tm), col),
            pl.BlockSpec((DIFF_W, tm), col),
            pl.BlockSpec((tm, DIFF_W), row),
            pl.BlockSpec((DIFF_W, tm), col),
        ],
        out_shape=[
            jax.ShapeDtypeStruct((t, GDN_QKV), F32),
            jax.ShapeDtypeStruct((t, GDN_Z), F32),
            jax.ShapeDtypeStruct((GDN_GATES, t), F32),
            jax.ShapeDtypeStruct((DIFF_W, t), BF16),
            jax.ShapeDtypeStruct((t, DIFF_W), BF16),
            jax.ShapeDtypeStruct((DIFF_W, t), BF16),
        ],
        compiler_params=_params("parallel"),
        name="inproj",
    )(x, wn, wt, cos_t, sin_t, kc, ks1, ks2)


def _unit_tri_inverses(ms, eye, same16, mid32, same32):
    b16 = lambda a: a.astype(BF16)
    ds = [b16(jnp.where(same16, m, 0.0)) for m in ms]
    c32s = [b16(jnp.where(mid32, m, 0.0)) for m in ms]
    c64s = [b16(jnp.where(same32, 0.0, m)) for m in ms]
    d2s = [b16(_dot(d, d)) for d in ds]
    d4s = [b16(_dot(d2, d2)) for d2 in d2s]
    d8s = [b16(_dot(d4, d4)) for d4 in d4s]
    ts = [eye - d for d in ds]
    for powers in (d2s, d4s, d8s):
        ts = [t + _dot(b16(t), p) for t, p in zip(ts, powers)]
    for corners in (c32s, c64s):
        tbs = [b16(t) for t in ts]
        inner = [b16(_dot(c, tb)) for c, tb in zip(corners, tbs)]
        ts = [t - _dot(tb, i) for t, tb, i in zip(ts, tbs, inner)]
    return ts


def _gdn_local_kernel(qkv_ref, prev_ref, next_ref, gt_ref, cw_ref, pcol_ref,
                      uf_ref, wf_ref, qdf_ref, kdf_ref, af_ref,
                      ub_ref, wb_ref, qdb_ref, kdb_ref, ab_ref, gl_ref, *, blocks_per_seq):
    sb = qkv_ref.shape[0]
    mat = GDN_MATRIX_ROWS
    nsub = sb // mat
    j = pl.program_id(0) % blocks_per_seq
    prev_on = j > 0
    next_on = j < blocks_per_seq - 1
    ext_rows = mat + 2 * SUBLANES
    nch = 2 * HEADS
    u_refs, w_refs, qd_refs, kd_refs, a_refs = ((uf_ref, ub_ref), (wf_ref, wb_ref), (qdf_ref, qdb_ref),
                                                (kdf_ref, kdb_ref), (af_ref, ab_ref))

    def conv_silu(sub):
        r0 = sub * mat
        out = []
        for cg in range(GDN_QKV // LANES):
            sl = slice(cg * LANES, (cg + 1) * LANES)
            before = jnp.where(prev_on, prev_ref[:, sl], 0.0) if sub == 0 else qkv_ref[r0 - SUBLANES:r0, sl]
            after = (jnp.where(next_on, next_ref[:, sl], 0.0) if sub == nsub - 1
                     else qkv_ref[r0 + mat:r0 + mat + SUBLANES, sl])
            ext = jnp.concatenate([before, qkv_ref[r0:r0 + mat, sl], after], axis=0)
            acc = None
            for tap in range(CONV_WIDTH):
                shift = (CONV_PAD - tap) % ext_rows
                rolled = ext if shift == 0 else pltpu.roll(ext, shift, axis=0)
                term = rolled[SUBLANES:SUBLANES + mat] * cw_ref[tap:tap + 1, sl]
                acc = term if acc is None else acc + term
            out.append(acc * _sigmoid(acc))
        return out

    def l2n(t):
        return t * lax.rsqrt(jnp.sum(t * t, -1, keepdims=True) + 1e-6)

    ri = lax.broadcasted_iota(jnp.int32, (mat, mat), 0)
    ci = lax.broadcasted_iota(jnp.int32, (mat, mat), 1)
    same64 = (ri // CHUNK) == (ci // CHUNK)
    same32 = (ri // 32) == (ci // 32)
    same16 = (ri // 16) == (ci // 16)
    mid32 = jnp.logical_and(same32, jnp.logical_not(same16))
    diag = ri == ci
    eye = jnp.where(diag, 1.0, 0.0).astype(F32)
    incl = (jnp.logical_and(same64, ci <= ri), jnp.logical_and(same64, ci >= ri))
    pos = lax.broadcasted_iota(jnp.int32, (nch, mat), 1) % CHUNK
    is_fwd = lax.broadcasted_iota(jnp.int32, (nch, mat), 0) < HEADS

    def wy_factors(sub, conv):
        r0 = sub * mat
        rows_all = slice(r0, r0 + mat)
        g_r = gt_ref[:, rows_all]
        beta_r = _sigmoid(g_r[:nch])
        gval = -jnp.exp(pcol_ref[nch:, 0:1]) * _softplus(g_r[nch:] + pcol_ref[nch:, 1:2])
        pre, suf = gval, gval
        step = 1
        while step < CHUNK:
            pre = pre + jnp.where(pos >= step, pltpu.roll(pre, step, axis=1), 0.0)
            suf = suf + jnp.where(pos < CHUNK - step, pltpu.roll(suf, mat - step, axis=1), 0.0)
            step *= 2
        gtot = pre + suf - gval
        gc_r = jnp.where(is_fwd, pre, suf)
        for c in range(mat // CHUNK):
            gl = jnp.exp(gtot[:, c * CHUNK:c * CHUNK + 1])
            g0 = (r0 // CHUNK + c) * SUBLANES
            gl_ref[g0:g0 + SUBLANES, :] = jnp.broadcast_to(gl, (SUBLANES, LANES))
        stack = jnp.concatenate([beta_r, gc_r, jnp.exp(gc_r), jnp.exp(gtot - gc_r),
                                 jnp.zeros((LANES - 4 * nch, mat), F32)], axis=0)
        cols = stack.T

        qs = [l2n(conv[h]) * (HEAD_DIM ** -0.5) for h in range(HEADS)]
        ks = [l2n(conv[HEADS + h]) for h in range(HEADS)]
        vs = [conv[2 * HEADS + h] for h in range(HEADS)]
        kbs = [k.astype(BF16) for k in ks]
        kks = [_dot_nt(kb, kb) for kb in kbs]
        qks = [_dot_nt(q.astype(BF16), kb) for q, kb in zip(qs, kbs)]

        chains = [(h, d) for h in range(HEADS) for d in range(2)]
        col = lambda group, h, d: cols[:, group * nch + d * HEADS + h:group * nch + d * HEADS + h + 1]
        ms = []
        for h, d in chains:
            r = d * HEADS + h
            decay = jnp.exp(jnp.where(incl[d], col(1, h, d) - gc_r[r:r + 1, :], -jnp.inf))
            ms.append(jnp.where(diag, 0.0, kks[h] * col(0, h, d) * decay))
            attn = (qks[h] * decay).astype(BF16)
            for c in range(mat // CHUNK):
                rows = slice(c * CHUNK, (c + 1) * CHUNK)
                a_refs[d][h, r0 + c * CHUNK:r0 + (c + 1) * CHUNK, :] = attn[rows, rows]
        tinvs = _unit_tri_inverses(ms, eye, same16, mid32, same32)
        rhss = [jnp.concatenate([vs[h] * col(0, h, d), ks[h] * (col(0, h, d) * col(2, h, d))], axis=1).astype(BF16)
                for h, d in chains]
        sols = [_dot(t.astype(BF16), rhs) for t, rhs in zip(tinvs, rhss)]
        for (h, d), sol in zip(chains, sols):
            hs = slice(h * HEAD_DIM, (h + 1) * HEAD_DIM)
            u_refs[d][rows_all, hs] = sol[:, :HEAD_DIM]
            w_refs[d][rows_all, hs] = sol[:, HEAD_DIM:].astype(BF16)
            qd_refs[d][rows_all, hs] = (qs[h] * col(2, h, d)).astype(BF16)
            kd_refs[d][rows_all, hs] = (ks[h] * col(3, h, d)).astype(BF16)

    for sub in range(nsub):
        wy_factors(sub, conv_silu(sub))


def _gdn_local(qkv, gates_t, conv_w8, pcol, seq):
    t = qkv.shape[0]
    sb = GDN_LOCAL_ROWS
    bps = seq // sb
    hb = sb // SUBLANES
    n8 = t // SUBLANES
    row = lambda i: (i, 0)
    const = lambda i: (0, 0)
    wide = jax.ShapeDtypeStruct((t, HEADS * HEAD_DIM), F32)
    wide_bf = jax.ShapeDtypeStruct((t, HEADS * HEAD_DIM), BF16)
    attn = jax.ShapeDtypeStruct((HEADS, t, CHUNK), BF16)
    wide_spec = pl.BlockSpec((sb, HEADS * HEAD_DIM), row)
    attn_spec = pl.BlockSpec((HEADS, sb, CHUNK), lambda i: (0, i, 0))
    dir_shapes = [wide, wide_bf, wide_bf, wide_bf, attn]
    dir_specs = [wide_spec, wide_spec, wide_spec, wide_spec, attn_spec]
    return pl.pallas_call(
        functools.partial(_gdn_local_kernel, blocks_per_seq=bps),
        grid=(t // sb,),
        in_specs=[
            pl.BlockSpec((sb, GDN_QKV), row),
            pl.BlockSpec((SUBLANES, GDN_QKV), lambda i: (jnp.maximum(i * hb - 1, 0), 0)),
            pl.BlockSpec((SUBLANES, GDN_QKV), lambda i: (jnp.minimum((i + 1) * hb, n8 - 1), 0)),
            pl.BlockSpec((GDN_GATES, sb), lambda i: (0, i)),
            pl.BlockSpec((SUBLANES, GDN_QKV), const),
            pl.BlockSpec((GDN_GATES, LANES), const),
        ],
        out_specs=dir_specs + dir_specs + [pl.BlockSpec((sb // CHUNK * SUBLANES, LANES), row)],
        out_shape=dir_shapes + dir_shapes + [jax.ShapeDtypeStruct((t // CHUNK * SUBLANES, LANES), F32)],
        compiler_params=_params("parallel"),
        name="gdn_local",
    )(qkv, qkv, qkv, gates_t, conv_w8, pcol)


def _gdn_scan_kernel(uf_ref, wf_ref, qdf_ref, kdf_ref, af_ref, glf_ref,
                     ub_ref, wb_ref, qdb_ref, kdb_ref, ab_ref, glb_ref,
                     of_ref, ob_ref, state_ref, *, chunks, group):
    @pl.when(pl.program_id(1) == 0)
    def _():
        state_ref[...] = jnp.zeros_like(state_ref)

    dirs = ((uf_ref, wf_ref, qdf_ref, kdf_ref, af_ref, glf_ref, of_ref),
            (ub_ref, wb_ref, qdb_ref, kdb_ref, ab_ref, glb_ref, ob_ref))
    chains = [(g, d, h) for g in range(group) for d in range(2) for h in range(HEADS)]

    def body(c, carry):
        chunk = (c, chunks - 1 - c)
        rows = tuple(pl.ds(pl.multiple_of(ch * CHUNK, CHUNK), CHUNK) for ch in chunk)
        hs = lambda h: slice(h * HEAD_DIM, (h + 1) * HEAD_DIM)
        slot = lambda g, d, h: (g * 2 + d) * HEADS + h
        ws_qs = []
        for g, d, h in chains:
            w_ref, qd_ref = dirs[d][1], dirs[d][2]
            lhs = jnp.concatenate([w_ref[g, rows[d], hs(h)], qd_ref[g, rows[d], hs(h)]], axis=0)
            ws_qs.append(_dot(lhs, state_ref[slot(g, d, h)].astype(BF16)))
        vbs = [(dirs[d][0][g, rows[d], hs(h)] - wq[:CHUNK]).astype(BF16) for (g, d, h), wq in zip(chains, ws_qs)]
        for (g, d, h), wq, vb in zip(chains, ws_qs, vbs):
            kd_ref, a_ref, gl_ref, o_ref = dirs[d][3], dirs[d][4], dirs[d][5], dirs[d][6]
            o_ref[g, rows[d], hs(h)] = wq[CHUNK:] + _dot(a_ref[h, g, rows[d], :], vb)
            gl = gl_ref[g, pl.ds(chunk[d] * SUBLANES + d * HEADS + h, 1), :]
            state_ref[slot(g, d, h)] = state_ref[slot(g, d, h)] * gl + _dot_tn(kd_ref[g, rows[d], hs(h)], vb)
        return carry

    lax.fori_loop(0, chunks, body, 0)


def _gdn_scan(local_out, batch, seq):
    cb = GDN_SCAN_CHUNKS
    grp = GDN_SCAN_GROUP
    rb = cb * CHUNK
    nb = seq // rb
    wide = lambda a: a.reshape(batch, seq, HEADS * HEAD_DIM)
    uf, wf, qdf, kdf, af, ub, wb, qdb, kdb, ab, gl = local_out
    uf, wf, qdf, kdf, ub, wb, qdb, kdb = (wide(a) for a in (uf, wf, qdf, kdf, ub, wb, qdb, kdb))
    af, ab = (a.reshape(HEADS, batch, seq, CHUNK) for a in (af, ab))
    gl = gl.reshape(batch, seq // CHUNK * SUBLANES, LANES)
    fwd = lambda b, j: (b, j, 0)
    bwd = lambda b, j: (b, nb - 1 - j, 0)
    fwd4 = lambda b, j: (0, b, j, 0)
    bwd4 = lambda b, j: (0, b, nb - 1 - j, 0)

    def dir_specs(m3, m4):
        blk = pl.BlockSpec((grp, rb, HEADS * HEAD_DIM), m3)
        return [blk, blk, blk, blk, pl.BlockSpec((HEADS, grp, rb, CHUNK), m4),
                pl.BlockSpec((grp, cb * SUBLANES, LANES), m3)]

    out = jax.ShapeDtypeStruct((batch, seq, HEADS * HEAD_DIM), F32)
    o_f, o_b = pl.pallas_call(
        functools.partial(_gdn_scan_kernel, chunks=cb, group=grp),
        grid=(batch // grp, nb),
        in_specs=dir_specs(fwd, fwd4) + dir_specs(bwd, bwd4),
        out_specs=[pl.BlockSpec((grp, rb, HEADS * HEAD_DIM), fwd), pl.BlockSpec((grp, rb, HEADS * HEAD_DIM), bwd)],
        out_shape=[out, out],
        scratch_shapes=[pltpu.VMEM((grp * 2 * HEADS, HEAD_DIM, HEAD_DIM), F32)],
        compiler_params=_params("parallel", "arbitrary"),
        name="gdn_scan",
    )(uf, wf, qdf, kdf, af, gl, ub, wb, qdb, kdb, ab, gl)
    return o_f.reshape(batch * seq, HEADS * HEAD_DIM), o_b.reshape(batch * seq, HEADS * HEAD_DIM)


def _attn_kernel(qt_ref, k_ref, vt_ref, lq_ref, g_ref, o_ref, s_ref, p_ref, acc_ref, *, seq, kblk, qb, lam_init):
    nblk = seq // kblk
    nchain = qt_ref.shape[1] // qb
    row = lax.broadcasted_iota(jnp.int32, (2 * DIFF_DQK, 2 * qb), 0)
    colq = lax.broadcasted_iota(jnp.int32, (2 * DIFF_DQK, 2 * qb), 1)
    keep = (row < DIFF_DQK) == (colq < qb)
    rhs = []
    for c in range(nchain):
        qt = qt_ref[:, c * qb:(c + 1) * qb]
        rhs.append(jnp.where(keep, jnp.concatenate([qt, qt], axis=1), jnp.zeros((), BF16)))
    ones = jnp.ones((ATTN_SUM_ROWS, kblk), BF16)

    def key_rows(i):
        return pl.ds(i * kblk, kblk)

    groups = [(g, slice(g * LANES, (g + 1) * LANES)) for g in range(2 * qb // LANES)]

    def score_block(c, i, slot):
        sc = _dot(k_ref[key_rows(i), :], rhs[c])
        for g, ls in groups:
            s_ref[c, slot, g] = sc[:, ls]

    def weighted_values(c, i, slot):
        lhs = jnp.concatenate([vt_ref[:, key_rows(i)], ones], axis=0)
        return _dot(lhs, jnp.concatenate([p_ref[c, slot, g] for g, _ in groups], axis=1))

    def step(c, i, m):
        if i > 0:
            pv = weighted_values(c, i - 1, (i - 1) % 2)
        if i + ATTN_AHEAD < nblk:
            score_block(c, i + ATTN_AHEAD, (i + ATTN_AHEAD) % ATTN_S_SLOTS)
        m_out = []
        for g, ls in groups:
            s = s_ref[c, i % ATTN_S_SLOTS, g]
            m_new = jnp.maximum(m[g], jnp.max(s, axis=0, keepdims=True))
            p_ref[c, i % 2, g] = jnp.exp2(s - m_new).astype(BF16)
            if i > 0:
                acc_ref[c, g] = (acc_ref[c, g] + pv[:, ls]) * jnp.exp2(m[g] - m_new)
            m_out.append(m_new)
        return m_out

    acc_ref[...] = jnp.zeros_like(acc_ref)
    for c in range(nchain):
        for i in range(ATTN_AHEAD):
            score_block(c, i, i)
        m = [jnp.full((1, LANES), -jnp.inf, F32)] * len(groups)
        for i in range(nblk):
            m = step(c, i, m)

    lq = lq_ref[...]
    lam = (jnp.exp(jnp.sum(lq[0:1] * lq[1:2], axis=-1, keepdims=True))
           - jnp.exp(jnp.sum(lq[2:3] * lq[3:4], axis=-1, keepdims=True)) + lam_init)
    for c in range(nchain):
        pv = weighted_values(c, nblk - 1, (nblk - 1) % 2)
        acc = jnp.concatenate([acc_ref[c, g] + pv[:, ls] for g, ls in groups], axis=1)
        on = acc[:HEAD_DIM] / acc[HEAD_DIM:HEAD_DIM + 1]
        o = (on[:, :qb] - lam * on[:, qb:]).T
        o = o * lax.rsqrt(jnp.mean(o * o, -1, keepdims=True) + 1e-6) * g_ref[...]
        o_ref[c * qb:(c + 1) * qb, :] = (o * (1.0 - lam_init)).astype(o_ref.dtype)


def _attention(qt, k, vt, lam_qk, norm_g, batch, seq, lam_init):
    t = k.shape[0]
    qb = ATTN_Q_BLOCK
    kblk = ATTN_K_BLOCK
    nc = ATTN_CHAINS
    nq = seq // (nc * qb)
    ng = 2 * qb // LANES
    return pl.pallas_call(
        functools.partial(_attn_kernel, seq=seq, kblk=kblk, qb=qb, lam_init=lam_init),
        grid=(batch, HEADS, nq),
        in_specs=[
            pl.BlockSpec((2 * DIFF_DQK, nc * qb), lambda b, h, i: (h, b * nq + i)),
            pl.BlockSpec((seq, 2 * DIFF_DQK), lambda b, h, i: (b, h)),
            pl.BlockSpec((HEAD_DIM, seq), lambda b, h, i: (h, b)),
            pl.BlockSpec((4, DIFF_DQK), lambda b, h, i: (0, 0)),
            pl.BlockSpec((1, HEAD_DIM), lambda b, h, i: (0, 0)),
        ],
        out_specs=pl.BlockSpec((nc * qb, HEAD_DIM), lambda b, h, i: (b * nq + i, h)),
        out_shape=jax.ShapeDtypeStruct((t, HEADS * HEAD_DIM), BF16),
        scratch_shapes=[pltpu.VMEM((nc, ATTN_S_SLOTS, ng, kblk, LANES), F32),
                        pltpu.VMEM((nc, 2, ng, kblk, LANES), BF16),
                        pltpu.VMEM((nc, ng, HEAD_DIM + ATTN_SUM_ROWS, LANES), F32)],
        compiler_params=_params("parallel", "parallel", "arbitrary"),
        name="diff_attn",
    )(qt, k, vt, lam_qk, norm_g)


def _mix_ffn_kernel(of_ref, ob_ref, z_ref, oa_ref, x_ref, wo_ref, gg_ref, l1g_ref, l1b_ref,
                    wg_ref, wu_ref, wd_ref, l2g_ref, l2b_ref, y_ref, *, alpha, parts):
    rows = [slice(i * (x_ref.shape[0] // parts), (i + 1) * (x_ref.shape[0] // parts)) for i in range(parts)]

    def mixed(rs):
        cols = []
        for h in range(HEADS):
            hs = slice(h * HEAD_DIM, (h + 1) * HEAD_DIM)
            o = of_ref[rs, hs] + ob_ref[rs, hs]
            o = o * lax.rsqrt(jnp.mean(o * o, -1, keepdims=True) + 1e-6) * gg_ref[...]
            zz = z_ref[rs, hs]
            cols.append((o * (zz * _sigmoid(zz))).astype(BF16))
        return jnp.concatenate(cols + [oa_ref[rs, :]], axis=1)

    mixes = [mixed(rs) for rs in rows]
    x1s = [_layer_norm(alpha * x_ref[rs, :] + _dot(mx, wo_ref[...]), l1g_ref[...], l1b_ref[...])
           for rs, mx in zip(rows, mixes)]
    xbs = [x1.astype(BF16) for x1 in x1s]
    gates = [_dot(xb, wg_ref[...]) for xb in xbs]
    ups = [_dot(xb, wu_ref[...]) for xb in xbs]
    acts = [(gate * _sigmoid(gate) * up).astype(BF16) for gate, up in zip(gates, ups)]
    for rs, x1, act in zip(rows, x1s, acts):
        y_ref[rs, :] = _layer_norm(alpha * x1 + _dot(act, wd_ref[...]), l2g_ref[...], l2b_ref[...])


def _mix_ffn(o_f, o_b, z, o_attn, x, p, alpha):
    t = x.shape[0]
    tm = ROW_TILE
    row = lambda i: (i, 0)
    const = lambda i: (0, 0)
    once = pl.Buffered(1)
    half = pl.BlockSpec((tm, HEADS * HEAD_DIM), row)
    vec = pl.BlockSpec((1, D_MODEL), const)
    return pl.pallas_call(
        functools.partial(_mix_ffn_kernel, alpha=alpha, parts=MIX_FFN_PARTS),
        grid=(t // tm,),
        in_specs=[half, half, half, half,
                  pl.BlockSpec((tm, D_MODEL), row),
                  pl.BlockSpec((D_MODEL, D_MODEL), const, pipeline_mode=once),
                  pl.BlockSpec((1, HEAD_DIM), const), vec, vec,
                  pl.BlockSpec((D_MODEL, D_FF), const, pipeline_mode=once),
                  pl.BlockSpec((D_MODEL, D_FF), const, pipeline_mode=once),
                  pl.BlockSpec((D_FF, D_MODEL), const, pipeline_mode=once),
                  vec, vec],
        out_specs=pl.BlockSpec((tm, D_MODEL), row),
        out_shape=jax.ShapeDtypeStruct((t, D_MODEL), F32),
        compiler_params=_params("parallel"),
        name="mix_ffn",
    )(o_f, o_b, z, o_attn, x, p["w_out"], p["gdn_g"], p["ln1_g"], p["ln1_b"],
      p["wg"], p["wu"], p["wd"], p["ln2_g"], p["ln2_b"])


def _rope_tables(seq):
    inv = 1.0 / (ROPE_THETA ** (jnp.arange(0, ROPE_DIM, 2, dtype=F32) / ROPE_DIM))
    ang = jnp.arange(seq, dtype=F32)[:, None] * inv[None, :]
    cos, sin = jnp.cos(ang), jnp.sin(ang)
    zero8 = jnp.zeros_like(sin)
    rest = DIFF_DQK - ROPE_DIM
    c64 = jnp.concatenate([cos, cos, jnp.ones((seq, rest), F32)], axis=1)
    s1 = jnp.concatenate([-sin, zero8, jnp.zeros((seq, rest), F32)], axis=1)
    s2 = jnp.concatenate([zero8, sin, jnp.zeros((seq, rest), F32)], axis=1)
    two = lambda a: jnp.concatenate([a, a], axis=1)
    return cos.T, sin.T, two(c64), two(s1), two(s2)


def _prep_layer(w_in, conv_w, a_log, dt_bias, gdn_norm_g, lam_qk, diff_norm_g, w_out, ln1_g, ln1_b,
                w_gate_up, w_down, ln2_g, ln2_b):
    c0 = GDN_QKV
    c1 = c0 + GDN_Z
    c2 = c1 + GDN_GATES
    c3 = c2 + DIFF_W
    c4 = c3 + DIFF_W
    w_g = w_in[:, c1:c2]
    wn = jnp.concatenate([w_in[:, :c1], w_in[:, c3:c4]], axis=1).astype(BF16)
    wt = jnp.concatenate([w_in[:, c2:c3], w_in[:, c4:], w_g], axis=1).T.astype(BF16)
    conv_w8 = jnp.pad(conv_w, ((0, SUBLANES - CONV_WIDTH), (0, 0)))
    gate_par = jnp.stack([a_log.reshape(-1), dt_bias.reshape(-1)])
    pcol = jnp.pad(gate_par.T, ((2 * HEADS, 0), (0, LANES - 2)))
    return dict(
        wn=wn, wt=wt, conv_w8=conv_w8, pcol=pcol,
        gdn_g=gdn_norm_g.reshape(1, HEAD_DIM), lam_qk=lam_qk, diff_g=diff_norm_g.reshape(1, HEAD_DIM),
        w_out=w_out.astype(BF16), ln1_g=ln1_g.reshape(1, D_MODEL), ln1_b=ln1_b.reshape(1, D_MODEL),
        wg=w_gate_up[:, :D_FF].astype(BF16), wu=w_gate_up[:, D_FF:].astype(BF16), wd=w_down.astype(BF16),
        ln2_g=ln2_g.reshape(1, D_MODEL), ln2_b=ln2_b.reshape(1, D_MODEL))


def _trunk(x, layers, alpha):
    batch, seq, _ = x.shape
    rope = _rope_tables(seq)
    xf = x.reshape(batch * seq, D_MODEL)
    for l, p in enumerate(layers):
        lam_init = 0.8 - 0.6 * math.exp(-0.3 * l)
        qkv, z, gates_t, qt, k, vt = _inproj(xf, p["wn"], p["wt"], rope, seq)
        local = _gdn_local(qkv, gates_t, p["conv_w8"], p["pcol"], seq)
        o_f, o_b = _gdn_scan(local, batch, seq)
        o_attn = _attention(qt, k, vt, p["lam_qk"], p["diff_g"], batch, seq, lam_init)
        xf = _mix_ffn(o_f, o_b, z, o_attn, xf, p, alpha)
    return xf.reshape(batch, seq, D_MODEL)


def kernel(x_prompt, x_sample, w_in, conv_w, a_log, dt_bias, gdn_norm_g, lam_qk, diff_norm_g, w_out,
           ln1_g, ln1_b, w_gate_up, w_down, ln2_g, ln2_b):
    depth = w_in.shape[0]
    alpha = (2 * depth) ** 0.25
    layers = [_prep_layer(w_in[l], conv_w[l], a_log[l], dt_bias[l], gdn_norm_g[l], lam_qk[l], diff_norm_g[l],
                          w_out[l], ln1_g[l], ln1_b[l], w_gate_up[l], w_down[l], ln2_g[l], ln2_b[l])
              for l in range(depth)]
    return (_trunk(x_prompt, layers, alpha), _trunk(x_sample, layers, alpha))
```

```python
import functools
import math

import jax
import jax.numpy as jnp
from jax import lax
from jax.experimental import pallas as pl
from jax.experimental.pallas import tpu as pltpu

F32 = jnp.float32
BF16 = jnp.bfloat16

D_MODEL = 1024
HEADS = 4
HEAD_DIM = 128
GDN_QKV = 3 * HEADS * HEAD_DIM
GDN_Z = HEADS * HEAD_DIM
GDN_GATES = 4 * HEADS
DIFF_DQK = 64
DIFF_W = HEADS * 2 * DIFF_DQK
CONV_WIDTH = 5
CONV_PAD = CONV_WIDTH // 2
CHUNK = 64
ROPE_THETA = 500000.0
ROPE_DIM = DIFF_DQK // 4
ROPE_HALF = ROPE_DIM // 2
D_FF = int(math.ceil(8 * D_MODEL / 3 / 256)) * 256

LANES = 128
SUBLANES = 8
VMEM_LIMIT = 56 * 1024 * 1024

ROW_TILE = 512
MIX_FFN_PARTS = 2
GDN_LOCAL_ROWS = 1024
GDN_MATRIX_ROWS = 256
GDN_SCAN_CHUNKS = 4
GDN_SCAN_GROUP = 4
ATTN_Q_BLOCK = 512
ATTN_CHAINS = 2
ATTN_AHEAD = 2
ATTN_S_SLOTS = 4
ATTN_K_BLOCK = 512
ATTN_SUM_ROWS = 16
LOG2E = 1.4426950408889634


def _params(*sem):
    return pltpu.CompilerParams(dimension_semantics=sem, vmem_limit_bytes=VMEM_LIMIT)


def _sigmoid(x):
    return 1.0 / (1.0 + jnp.exp(-x))


def _softplus(x):
    return jnp.maximum(x, 0.0) + jnp.log(1.0 + jnp.exp(-jnp.abs(x)))


def _dot(a, b):
    return jnp.dot(a, b, preferred_element_type=F32)


def _dot_nt(a, b):
    return lax.dot_general(a, b, (((1,), (1,)), ((), ())), preferred_element_type=F32)


def _dot_tn(a, b):
    return lax.dot_general(a, b, (((0,), (0,)), ((), ())), preferred_element_type=F32)


def _layer_norm(y, g, b):
    mu = jnp.mean(y, -1, keepdims=True)
    d = y - mu
    var = jnp.mean(d * d, -1, keepdims=True)
    return d * lax.rsqrt(var + 1e-5) * g + b


def _inproj_kernel(x_ref, wn_ref, wt_ref, cos_ref, sin_ref, kc_ref, ks1_ref, ks2_ref,
                   qkv_ref, z_ref, gt_ref, qt_ref, k_ref, vt_ref):
    xb = x_ref[...].astype(BF16)
    hn = _dot(xb, wn_ref[...])
    qkv_ref[...] = hn[:, :GDN_QKV]
    z_ref[...] = hn[:, GDN_QKV:GDN_QKV + GDN_Z]
    dk0 = GDN_QKV + GDN_Z
    kc, ks1, ks2 = kc_ref[...], ks1_ref[...], ks2_ref[...]
    for h in range(HEADS):
        kh = hn[:, dk0 + h * LANES:dk0 + (h + 1) * LANES]
        up = pltpu.roll(kh, LANES - ROPE_HALF, axis=1)
        dn = pltpu.roll(kh, ROPE_HALF, axis=1)
        k_ref[:, h * LANES:(h + 1) * LANES] = (kh * kc + up * ks1 + dn * ks2).astype(BF16)
    ht = _dot_nt(wt_ref[...], xb)
    vt_ref[...] = ht[DIFF_W:2 * DIFF_W].astype(BF16)
    gt_ref[...] = ht[2 * DIFF_W:2 * DIFF_W + GDN_GATES]
    cos, sin = cos_ref[...], sin_ref[...]
    pieces = []
    for c in range(DIFF_W // DIFF_DQK):
        r = c * DIFF_DQK
        a, b = ht[r:r + ROPE_HALF], ht[r + ROPE_HALF:r + ROPE_DIM]
        pieces += [a * cos - b * sin, b * cos + a * sin, ht[r + ROPE_DIM:r + DIFF_DQK]]
    qt_ref[...] = (jnp.concatenate(pieces, axis=0) * (DIFF_DQK ** -0.5 * LOG2E)).astype(BF16)


def _inproj(x, wn, wt, rope, seq):
    t = x.shape[0]
    tm = ROW_TILE
    nps = seq // tm
    cos_t, sin_t, kc, ks1, ks2 = rope
    row = lambda i: (i, 0)
    col = lambda i: (0, i)
    const = lambda i: (0, 0)
    pos_row = lambda i: (i % nps, 0)
    pos_col = lambda i: (0, i % nps)
    n_cols = wn.shape[1]
    t_rows = wt.shape[0]
    return pl.pallas_call(
        _inproj_kernel,
        grid=(t // tm,),
        in_specs=[
            pl.BlockSpec((tm, D_MODEL), row),
            pl.BlockSpec((D_MODEL, n_cols), const),
            pl.BlockSpec((t_rows, D_MODEL), const),
            pl.BlockSpec((ROPE_HALF, tm), pos_col),
            pl.BlockSpec((ROPE_HALF, tm), pos_col),
            pl.BlockSpec((tm, LANES), pos_row),
            pl.BlockSpec((tm, LANES), pos_row),
            pl.BlockSpec((tm, LANES), pos_row),
        ],
        out_specs=[
            pl.BlockSpec((tm, GDN_QKV), row),
            pl.BlockSpec((tm, GDN_Z), row),
            pl.BlockSpec((GDN_GATES, tm), col),
            pl.BlockSpec((DIFF_W, tm), col),
            pl.BlockSpec((tm, DIFF_W), row),
            pl.BlockSpec((DIFF_W, tm), col),
        ],
        out_shape=[
            jax.ShapeDtypeStruct((t, GDN_QKV), F32),
            jax.ShapeDtypeStruct((t, GDN_Z), F32),
            jax.ShapeDtypeStruct((GDN_GATES, t), F32),
            jax.ShapeDtypeStruct((DIFF_W, t), BF16),
            jax.ShapeDtypeStruct((t, DIFF_W), BF16),
            jax.ShapeDtypeStruct((DIFF_W, t), BF16),
        ],
        compiler_params=_params("parallel"),
        name="inproj",
    )(x, wn, wt, cos_t, sin_t, kc, ks1, ks2)


def _unit_tri_inverses(ms, eye, same16, mid32, same32):
    b16 = lambda a: a.astype(BF16)
    ds = [b16(jnp.where(same16, m, 0.0)) for m in ms]
    c32s = [b16(jnp.where(mid32, m, 0.0)) for m in ms]
    c64s = [b16(jnp.where(same32, 0.0, m)) for m in ms]
    d2s = [b16(_dot(d, d)) for d in ds]
    d4s = [b16(_dot(d2, d2)) for d2 in d2s]
    d8s = [b16(_dot(d4, d4)) for d4 in d4s]
    ts = [eye - d for d in ds]
    for powers in (d2s, d4s, d8s):
        ts = [t + _dot(b16(t), p) for t, p in zip(ts, powers)]
    for corners in (c32s, c64s):
        tbs = [b16(t) for t in ts]
        inner = [b16(_dot(c, tb)) for c, tb in zip(corners, tbs)]
        ts = [t - _dot(tb, i) for t, tb, i in zip(ts, tbs, inner)]
    return ts


def _gdn_local_kernel(qkv_ref, prev_ref, next_ref, gt_ref, cw_ref, pcol_ref,
                      uf_ref, wf_ref, qdf_ref, kdf_ref, af_ref,
                      ub_ref, wb_ref, qdb_ref, kdb_ref, ab_ref, gl_ref, *, blocks_per_seq):
    sb = qkv_ref.shape[0]
    mat = GDN_MATRIX_ROWS
    nsub = sb // mat
    j = pl.program_id(0) % blocks_per_seq
    prev_on = j > 0
    next_on = j < blocks_per_seq - 1
    ext_rows = mat + 2 * SUBLANES
    nch = 2 * HEADS
    u_refs, w_refs, qd_refs, kd_refs, a_refs = ((uf_ref, ub_ref), (wf_ref, wb_ref), (qdf_ref, qdb_ref),
                                                (kdf_ref, kdb_ref), (af_ref, ab_ref))

    def conv_silu(sub):
        r0 = sub * mat
        out = []
        for cg in range(GDN_QKV // LANES):
            sl = slice(cg * LANES, (cg + 1) * LANES)
            before = jnp.where(prev_on, prev_ref[:, sl], 0.0) if sub == 0 else qkv_ref[r0 - SUBLANES:r0, sl]
            after = (jnp.where(next_on, next_ref[:, sl], 0.0) if sub == nsub - 1
                     else qkv_ref[r0 + mat:r0 + mat + SUBLANES, sl])
            ext = jnp.concatenate([before, qkv_ref[r0:r0 + mat, sl], after], axis=0)
            acc = None
            for tap in range(CONV_WIDTH):
                shift = (CONV_PAD - tap) % ext_rows
                rolled = ext if shift == 0 else pltpu.roll(ext, shift, axis=0)
                term = rolled[SUBLANES:SUBLANES + mat] * cw_ref[tap:tap + 1, sl]
                acc = term if acc is None else acc + term
            out.append(acc * _sigmoid(acc))
        return out

    def l2n(t):
        return t * lax.rsqrt(jnp.sum(t * t, -1, keepdims=True) + 1e-6)

    ri = lax.broadcasted_iota(jnp.int32, (mat, mat), 0)
    ci = lax.broadcasted_iota(jnp.int32, (mat, mat), 1)
    same64 = (ri // CHUNK) == (ci // CHUNK)
    same32 = (ri // 32) == (ci // 32)
    same16 = (ri // 16) == (ci // 16)
    mid32 = jnp.logical_and(same32, jnp.logical_not(same16))
    diag = ri == ci
    eye = jnp.where(diag, 1.0, 0.0).astype(F32)
    incl = (jnp.logical_and(same64, ci <= ri), jnp.logical_and(same64, ci >= ri))
    pos = lax.broadcasted_iota(jnp.int32, (nch, mat), 1) % CHUNK
    is_fwd = lax.broadcasted_iota(jnp.int32, (nch, mat), 0) < HEADS

    def wy_factors(sub, conv):
        r0 = sub * mat
        rows_all = slice(r0, r0 + mat)
        g_r = gt_ref[:, rows_all]
        beta_r = _sigmoid(g_r[:nch])
        gval = -jnp.exp(pcol_ref[nch:, 0:1]) * _softplus(g_r[nch:] + pcol_ref[nch:, 1:2])
        pre, suf = gval, gval
        step = 1
        while step < CHUNK:
            pre = pre + jnp.where(pos >= step, pltpu.roll(pre, step, axis=1), 0.0)
            suf = suf + jnp.where(pos < CHUNK - step, pltpu.roll(suf, mat - step, axis=1), 0.0)
            step *= 2
        gtot = pre + suf - gval
        gc_r = jnp.where(is_fwd, pre, suf)
        for c in range(mat // CHUNK):
            gl = jnp.exp(gtot[:, c * CHUNK:c * CHUNK + 1])
            g0 = (r0 // CHUNK + c) * SUBLANES
            gl_ref[g0:g0 + SUBLANES, :] = jnp.broadcast_to(gl, (SUBLANES, LANES))
        stack = jnp.concatenate([beta_r, gc_r, jnp.exp(gc_r), jnp.exp(gtot - gc_r),
                                 jnp.zeros((LANES - 4 * nch, mat), F32)], axis=0)
        cols = stack.T

        qs = [l2n(conv[h]) * (HEAD_DIM ** -0.5) for h in range(HEADS)]
        ks = [l2n(conv[HEADS + h]) for h in range(HEADS)]
        vs = [conv[2 * HEADS + h] for h in range(HEADS)]
        kbs = [k.astype(BF16) for k in ks]
        kks = [_dot_nt(kb, kb) for kb in kbs]
        qks = [_dot_nt(q.astype(BF16), kb) for q, kb in zip(qs, kbs)]

        chains = [(h, d) for h in range(HEADS) for d in range(2)]
        col = lambda group, h, d: cols[:, group * nch + d * HEADS + h:group * nch + d * HEADS + h + 1]
        ms = []
        for h, d in chains:
            r = d * HEADS + h
            decay = jnp.exp(jnp.where(incl[d], col(1, h, d) - gc_r[r:r + 1, :], -jnp.inf))
            ms.append(jnp.where(diag, 0.0, kks[h] * col(0, h, d) * decay))
            attn = (qks[h] * decay).astype(BF16)
            for c in range(mat // CHUNK):
                rows = slice(c * CHUNK, (c + 1) * CHUNK)
                a_refs[d][h, r0 + c * CHUNK:r0 + (c + 1) * CHUNK, :] = attn[rows, rows]
        tinvs = _unit_tri_inverses(ms, eye, same16, mid32, same32)
        rhss = [jnp.concatenate([vs[h] * col(0, h, d), ks[h] * (col(0, h, d) * col(2, h, d))], axis=1).astype(BF16)
                for h, d in chains]
        sols = [_dot(t.astype(BF16), rhs) for t, rhs in zip(tinvs, rhss)]
        for (h, d), sol in zip(chains, sols):
            hs = slice(h * HEAD_DIM, (h + 1) * HEAD_DIM)
            u_refs[d][rows_all, hs] = sol[:, :HEAD_DIM]
            w_refs[d][rows_all, hs] = sol[:, HEAD_DIM:].astype(BF16)
            qd_refs[d][rows_all, hs] = (qs[h] * col(2, h, d)).astype(BF16)
            kd_refs[d][rows_all, hs] = (ks[h] * col(3, h, d)).astype(BF16)

    for sub in range(nsub):
        wy_factors(sub, conv_silu(sub))


def _gdn_local(qkv, gates_t, conv_w8, pcol, seq):
    t = qkv.shape[0]
    sb = GDN_LOCAL_ROWS
    bps = seq // sb
    hb = sb // SUBLANES
    n8 = t // SUBLANES
    row = lambda i: (i, 0)
    const = lambda i: (0, 0)
    wide = jax.ShapeDtypeStruct((t, HEADS * HEAD_DIM), F32)
    wide_bf = jax.ShapeDtypeStruct((t, HEADS * HEAD_DIM), BF16)
    attn = jax.ShapeDtypeStruct((HEADS, t, CHUNK), BF16)
    wide_spec = pl.BlockSpec((sb, HEADS * HEAD_DIM), row)
    attn_spec = pl.BlockSpec((HEADS, sb, CHUNK), lambda i: (0, i, 0))
    dir_shapes = [wide, wide_bf, wide_bf, wide_bf, attn]
    dir_specs = [wide_spec, wide_spec, wide_spec, wide_spec, attn_spec]
    return pl.pallas_call(
        functools.partial(_gdn_local_kernel, blocks_per_seq=bps),
        grid=(t // sb,),
        in_specs=[
            pl.BlockSpec((sb, GDN_QKV), row),
            pl.BlockSpec((SUBLANES, GDN_QKV), lambda i: (jnp.maximum(i * hb - 1, 0), 0)),
            pl.BlockSpec((SUBLANES, GDN_QKV), lambda i: (jnp.minimum((i + 1) * hb, n8 - 1), 0)),
            pl.BlockSpec((GDN_GATES, sb), lambda i: (0, i)),
            pl.BlockSpec((SUBLANES, GDN_QKV), const),
            pl.BlockSpec((GDN_GATES, LANES), const),
        ],
        out_specs=dir_specs + dir_specs + [pl.BlockSpec((sb // CHUNK * SUBLANES, LANES), row)],
        out_shape=dir_shapes + dir_shapes + [jax.ShapeDtypeStruct((t // CHUNK * SUBLANES, LANES), F32)],
        compiler_params=_params("parallel"),
        name="gdn_local",
    )(qkv, qkv, qkv, gates_t, conv_w8, pcol)


def _gdn_scan_kernel(uf_ref, wf_ref, qdf_ref, kdf_ref, af_ref, glf_ref,
                     ub_ref, wb_ref, qdb_ref, kdb_ref, ab_ref, glb_ref,
                     of_ref, ob_ref, state_ref, *, chunks, group):
    @pl.when(pl.program_id(1) == 0)
    def _():
        state_ref[...] = jnp.zeros_like(state_ref)

    dirs = ((uf_ref, wf_ref, qdf_ref, kdf_ref, af_ref, glf_ref, of_ref),
            (ub_ref, wb_ref, qdb_ref, kdb_ref, ab_ref, glb_ref, ob_ref))
    chains = [(g, d, h) for g in range(group) for d in range(2) for h in range(HEADS)]

    def body(c, carry):
        chunk = (c, chunks - 1 - c)
        rows = tuple(pl.ds(pl.multiple_of(ch * CHUNK, CHUNK), CHUNK) for ch in chunk)
        hs = lambda h: slice(h * HEAD_DIM, (h + 1) * HEAD_DIM)
        slot = lambda g, d, h: (g * 2 + d) * HEADS + h
        ws_qs = []
        for g, d, h in chains:
            w_ref, qd_ref = dirs[d][1], dirs[d][2]
            lhs = jnp.concatenate([w_ref[g, rows[d], hs(h)], qd_ref[g, rows[d], hs(h)]], axis=0)
            ws_qs.append(_dot(lhs, state_ref[slot(g, d, h)].astype(BF16)))
        vbs = [(dirs[d][0][g, rows[d], hs(h)] - wq[:CHUNK]).astype(BF16) for (g, d, h), wq in zip(chains, ws_qs)]
        for (g, d, h), wq, vb in zip(chains, ws_qs, vbs):
            kd_ref, a_ref, gl_ref, o_ref = dirs[d][3], dirs[d][4], dirs[d][5], dirs[d][6]
            o_ref[g, rows[d], hs(h)] = wq[CHUNK:] + _dot(a_ref[h, g, rows[d], :], vb)
            gl = gl_ref[g, pl.ds(chunk[d] * SUBLANES + d * HEADS + h, 1), :]
            state_ref[slot(g, d, h)] = state_ref[slot(g, d, h)] * gl + _dot_tn(kd_ref[g, rows[d], hs(h)], vb)
        return carry

    lax.fori_loop(0, chunks, body, 0)


def _gdn_scan(local_out, batch, seq):
    cb = GDN_SCAN_CHUNKS
    grp = GDN_SCAN_GROUP
    rb = cb * CHUNK
    nb = seq // rb
    wide = lambda a: a.reshape(batch, seq, HEADS * HEAD_DIM)
    uf, wf, qdf, kdf, af, ub, wb, qdb, kdb, ab, gl = local_out
    uf, wf, qdf, kdf, ub, wb, qdb, kdb = (wide(a) for a in (uf, wf, qdf, kdf, ub, wb, qdb, kdb))
    af, ab = (a.reshape(HEADS, batch, seq, CHUNK) for a in (af, ab))
    gl = gl.reshape(batch, seq // CHUNK * SUBLANES, LANES)
    fwd = lambda b, j: (b, j, 0)
    bwd = lambda b, j: (b, nb - 1 - j, 0)
    fwd4 = lambda b, j: (0, b, j, 0)
    bwd4 = lambda b, j: (0, b, nb - 1 - j, 0)

    def dir_specs(m3, m4):
        blk = pl.BlockSpec((grp, rb, HEADS * HEAD_DIM), m3)
        return [blk, blk, blk, blk, pl.BlockSpec((HEADS, grp, rb, CHUNK), m4),
                pl.BlockSpec((grp, cb * SUBLANES, LANES), m3)]

    out = jax.ShapeDtypeStruct((batch, seq, HEADS * HEAD_DIM), F32)
    o_f, o_b = pl.pallas_call(
        functools.partial(_gdn_scan_kernel, chunks=cb, group=grp),
        grid=(batch // grp, nb),
        in_specs=dir_specs(fwd, fwd4) + dir_specs(bwd, bwd4),
        out_specs=[pl.BlockSpec((grp, rb, HEADS * HEAD_DIM), fwd), pl.BlockSpec((grp, rb, HEADS * HEAD_DIM), bwd)],
        out_shape=[out, out],
        scratch_shapes=[pltpu.VMEM((grp * 2 * HEADS, HEAD_DIM, HEAD_DIM), F32)],
        compiler_params=_params("parallel", "arbitrary"),
        name="gdn_scan",
    )(uf, wf, qdf, kdf, af, gl, ub, wb, qdb, kdb, ab, gl)
    return o_f.reshape(batch * seq, HEADS * HEAD_DIM), o_b.reshape(batch * seq, HEADS * HEAD_DIM)


def _attn_kernel(qt_ref, k_ref, vt_ref, lq_ref, g_ref, o_ref, s_ref, p_ref, acc_ref, *, seq, kblk, qb, lam_init):
    nblk = seq // kblk
    nchain = qt_ref.shape[1] // qb
    row = lax.broadcasted_iota(jnp.int32, (2 * DIFF_DQK, 2 * qb), 0)
    colq = lax.broadcasted_iota(jnp.int32, (2 * DIFF_DQK, 2 * qb), 1)
    keep = (row < DIFF_DQK) == (colq < qb)
    rhs = []
    for c in range(nchain):
        qt = qt_ref[:, c * qb:(c + 1) * qb]
        rhs.append(jnp.where(keep, jnp.concatenate([qt, qt], axis=1), jnp.zeros((), BF16)))
    ones = jnp.ones((ATTN_SUM_ROWS, kblk), BF16)
    lq = lq_ref[...]
    lam = (jnp.exp(jnp.sum(lq[0:1] * lq[1:2], axis=-1, keepdims=True))
           - jnp.exp(jnp.sum(lq[2:3] * lq[3:4], axis=-1, keepdims=True)) + lam_init)

    def key_rows(i):
        return pl.ds(i * kblk, kblk)

    groups = [(g, slice(g * LANES, (g + 1) * LANES)) for g in range(2 * qb // LANES)]

    def score_block(c, i, slot):
        sc = _dot(k_ref[key_rows(i), :], rhs[c])
        for g, ls in groups:
            s_ref[c, slot, g, :kblk] = sc[:, ls]

    def weighted_values(c, i, slot):
        lhs = jnp.concatenate([vt_ref[:, key_rows(i)], ones], axis=0)
        return _dot(lhs, jnp.concatenate([p_ref[c, slot, g, :kblk] for g, _ in groups], axis=1))

    def start(c):
        for i in range(ATTN_AHEAD):
            score_block(c, i, i)
        return [jnp.full((1, LANES), -jnp.inf, F32)] * len(groups)

    def step(c, i, m):
        if i > 0:
            pv = weighted_values(c, i - 1, (i - 1) % 2)
        if i + ATTN_AHEAD < nblk:
            score_block(c, i + ATTN_AHEAD, (i + ATTN_AHEAD) % ATTN_S_SLOTS)
        m_out = []
        for g, ls in groups:
            s = s_ref[c, i % ATTN_S_SLOTS, g, :kblk]
            m_new = jnp.maximum(m[g], jnp.max(s, axis=0, keepdims=True))
            p_ref[c, i % 2, g, :kblk] = jnp.exp2(s - m_new).astype(BF16)
            if i > 0:
                alpha = jnp.exp2(m[g] - m_new)
                acc_ref[c, g] = (pv[:, ls] if i == 1 else acc_ref[c, g] + pv[:, ls]) * alpha
            m_out.append(m_new)
        return m_out

    def finish(c):
        pv = weighted_values(c, nblk - 1, (nblk - 1) % 2)
        acc = jnp.concatenate([acc_ref[c, g] + pv[:, ls] for g, ls in groups], axis=1)
        on = acc[:HEAD_DIM] / acc[HEAD_DIM:HEAD_DIM + 1]
        o = (on[:, :qb] - lam * on[:, qb:]).T
        o = o * lax.rsqrt(jnp.mean(o * o, -1, keepdims=True) + 1e-6) * g_ref[...]
        o_ref[c * qb:(c + 1) * qb, :] = (o * (1.0 - lam_init)).astype(o_ref.dtype)

    state = start(0)
    for c in range(nchain):
        for i in range(nblk):
            if c + 1 < nchain and i == nblk - ATTN_AHEAD:
                upcoming = start(c + 1)
            state = step(c, i, state)
        finish(c)
        if c + 1 < nchain:
            state = upcoming


def _attention(qt, k, vt, lam_qk, norm_g, batch, seq, lam_init):
    t = k.shape[0]
    qb = ATTN_Q_BLOCK
    kblk = ATTN_K_BLOCK
    nc = ATTN_CHAINS
    nq = seq // (nc * qb)
    ng = 2 * qb // LANES
    return pl.pallas_call(
        functools.partial(_attn_kernel, seq=seq, kblk=kblk, qb=qb, lam_init=lam_init),
        grid=(batch, HEADS, nq),
        in_specs=[
            pl.BlockSpec((2 * DIFF_DQK, nc * qb), lambda b, h, i: (h, b * nq + i)),
            pl.BlockSpec((seq, 2 * DIFF_DQK), lambda b, h, i: (b, h)),
            pl.BlockSpec((HEAD_DIM, seq), lambda b, h, i: (h, b)),
            pl.BlockSpec((4, DIFF_DQK), lambda b, h, i: (0, 0)),
            pl.BlockSpec((1, HEAD_DIM), lambda b, h, i: (0, 0)),
        ],
        out_specs=pl.BlockSpec((nc * qb, HEAD_DIM), lambda b, h, i: (b * nq + i, h)),
        out_shape=jax.ShapeDtypeStruct((t, HEADS * HEAD_DIM), BF16),
        scratch_shapes=[pltpu.VMEM((nc, ATTN_S_SLOTS, ng, kblk + SUBLANES, LANES), F32),
                        pltpu.VMEM((nc, 2, ng, kblk + 2 * SUBLANES, LANES), BF16),
                        pltpu.VMEM((nc, ng, HEAD_DIM + ATTN_SUM_ROWS, LANES), F32)],
        compiler_params=_params("parallel", "parallel", "arbitrary"),
        name="diff_attn",
    )(qt, k, vt, lam_qk, norm_g)


def _mix_ffn_kernel(of_ref, ob_ref, z_ref, oa_ref, x_ref, wo_ref, gg_ref, l1g_ref, l1b_ref,
                    wg_ref, wu_ref, wd_ref, l2g_ref, l2b_ref, y_ref, *, alpha, parts):
    rows = [slice(i * (x_ref.shape[0] // parts), (i + 1) * (x_ref.shape[0] // parts)) for i in range(parts)]

    def mixed(rs):
        cols = []
        for h in range(HEADS):
            hs = slice(h * HEAD_DIM, (h + 1) * HEAD_DIM)
            o = of_ref[rs, hs] + ob_ref[rs, hs]
            o = o * lax.rsqrt(jnp.mean(o * o, -1, keepdims=True) + 1e-6) * gg_ref[...]
            zz = z_ref[rs, hs]
            cols.append((o * (zz * _sigmoid(zz))).astype(BF16))
        return jnp.concatenate(cols + [oa_ref[rs, :]], axis=1)

    mixes = [mixed(rs) for rs in rows]
    x1s = [_layer_norm(alpha * x_ref[rs, :] + _dot(mx, wo_ref[...]), l1g_ref[...], l1b_ref[...])
           for rs, mx in zip(rows, mixes)]
    xbs = [x1.astype(BF16) for x1 in x1s]
    gates = [_dot(xb, wg_ref[...]) for xb in xbs]
    ups = [_dot(xb, wu_ref[...]) for xb in xbs]
    acts = [(gate * _sigmoid(gate) * up).astype(BF16) for gate, up in zip(gates, ups)]
    for rs, x1, act in zip(rows, x1s, acts):
        y_ref[rs, :] = _layer_norm(alpha * x1 + _dot(act, wd_ref[...]), l2g_ref[...], l2b_ref[...])


def _mix_ffn(o_f, o_b, z, o_attn, x, p, alpha):
    t = x.shape[0]
    tm = ROW_TILE
    row = lambda i: (i, 0)
    const = lambda i: (0, 0)
    once = pl.Buffered(1)
    half = pl.BlockSpec((tm, HEADS * HEAD_DIM), row)
    vec = pl.BlockSpec((1, D_MODEL), const)
    return pl.pallas_call(
        functools.partial(_mix_ffn_kernel, alpha=alpha, parts=MIX_FFN_PARTS),
        grid=(t // tm,),
        in_specs=[half, half, half, half,
                  pl.BlockSpec((tm, D_MODEL), row),
                  pl.BlockSpec((D_MODEL, D_MODEL), const, pipeline_mode=once),
                  pl.BlockSpec((1, HEAD_DIM), const), vec, vec,
                  pl.BlockSpec((D_MODEL, D_FF), const, pipeline_mode=once),
                  pl.BlockSpec((D_MODEL, D_FF), const, pipeline_mode=once),
                  pl.BlockSpec((D_FF, D_MODEL), const, pipeline_mode=once),
                  vec, vec],
        out_specs=pl.BlockSpec((tm, D_MODEL), row),
        out_shape=jax.ShapeDtypeStruct((t, D_MODEL), F32),
        compiler_params=_params("parallel"),
        name="mix_ffn",
    )(o_f, o_b, z, o_attn, x, p["w_out"], p["gdn_g"], p["ln1_g"], p["ln1_b"],
      p["wg"], p["wu"], p["wd"], p["ln2_g"], p["ln2_b"])


def _rope_tables(seq):
    inv = 1.0 / (ROPE_THETA ** (jnp.arange(0, ROPE_DIM, 2, dtype=F32) / ROPE_DIM))
    ang = jnp.arange(seq, dtype=F32)[:, None] * inv[None, :]
    cos, sin = jnp.cos(ang), jnp.sin(ang)
    zero8 = jnp.zeros_like(sin)
    rest = DIFF_DQK - ROPE_DIM
    c64 = jnp.concatenate([cos, cos, jnp.ones((seq, rest), F32)], axis=1)
    s1 = jnp.concatenate([-sin, zero8, jnp.zeros((seq, rest), F32)], axis=1)
    s2 = jnp.concatenate([zero8, sin, jnp.zeros((seq, rest), F32)], axis=1)
    two = lambda a: jnp.concatenate([a, a], axis=1)
    return cos.T, sin.T, two(c64), two(s1), two(s2)


def _prep_layer(w_in, conv_w, a_log, dt_bias, gdn_norm_g, lam_qk, diff_norm_g, w_out, ln1_g, ln1_b,
                w_gate_up, w_down, ln2_g, ln2_b):
    c0 = GDN_QKV
    c1 = c0 + GDN_Z
    c2 = c1 + GDN_GATES
    c3 = c2 + DIFF_W
    c4 = c3 + DIFF_W
    w_g = w_in[:, c1:c2]
    wn = jnp.concatenate([w_in[:, :c1], w_in[:, c3:c4]], axis=1).astype(BF16)
    wt = jnp.concatenate([w_in[:, c2:c3], w_in[:, c4:], w_g], axis=1).T.astype(BF16)
    conv_w8 = jnp.pad(conv_w, ((0, SUBLANES - CONV_WIDTH), (0, 0)))
    gate_par = jnp.stack([a_log.reshape(-1), dt_bias.reshape(-1)])
    pcol = jnp.pad(gate_par.T, ((2 * HEADS, 0), (0, LANES - 2)))
    return dict(
        wn=wn, wt=wt, conv_w8=conv_w8, pcol=pcol,
        gdn_g=gdn_norm_g.reshape(1, HEAD_DIM), lam_qk=lam_qk, diff_g=diff_norm_g.reshape(1, HEAD_DIM),
        w_out=w_out.astype(BF16), ln1_g=ln1_g.reshape(1, D_MODEL), ln1_b=ln1_b.reshape(1, D_MODEL),
        wg=w_gate_up[:, :D_FF].astype(BF16), wu=w_gate_up[:, D_FF:].astype(BF16), wd=w_down.astype(BF16),
        ln2_g=ln2_g.reshape(1, D_MODEL), ln2_b=ln2_b.reshape(1, D_MODEL))


def _trunk(x, layers, alpha):
    batch, seq, _ = x.shape
    rope = _rope_tables(seq)
    xf = x.reshape(batch * seq, D_MODEL)
    for l, p in enumerate(layers):
        lam_init = 0.8 - 0.6 * math.exp(-0.3 * l)
        qkv, z, gates_t, qt, k, vt = _inproj(xf, p["wn"], p["wt"], rope, seq)
        local = _gdn_local(qkv, gates_t, p["conv_w8"], p["pcol"], seq)
        o_f, o_b = _gdn_scan(local, batch, seq)
        o_attn = _attention(qt, k, vt, p["lam_qk"], p["diff_g"], batch, seq, lam_init)
        xf = _mix_ffn(o_f, o_b, z, o_attn, xf, p, alpha)
    return xf.reshape(batch, seq, D_MODEL)


def kernel(x_prompt, x_sample, w_in, conv_w, a_log, dt_bias, gdn_norm_g, lam_qk, diff_norm_g, w_out,
           ln1_g, ln1_b, w_gate_up, w_down, ln2_g, ln2_b):
    depth = w_in.shape[0]
    alpha = (2 * depth) ** 0.25
    layers = [_prep_layer(w_in[l], conv_w[l], a_log[l], dt_bias[l], gdn_norm_g[l], lam_qk[l], diff_norm_g[l],
                          w_out[l], ln1_g[l], ln1_b[l], w_gate_up[l], w_down[l], ln2_g[l], ln2_b[l])
              for l in range(depth)]
    return (_trunk(x_prompt, layers, alpha), _trunk(x_sample, layers, alpha))
```

```python
import functools
import math

import jax
import jax.numpy as jnp
from jax import lax
from jax.experimental import pallas as pl
from jax.experimental.pallas import tpu as pltpu

F32 = jnp.float32
BF16 = jnp.bfloat16

D_MODEL = 1024
HEADS = 4
HEAD_DIM = 128
GDN_QKV = 3 * HEADS * HEAD_DIM
GDN_Z = HEADS * HEAD_DIM
GDN_GATES = 4 * HEADS
DIFF_DQK = 64
DIFF_W = HEADS * 2 * DIFF_DQK
CONV_WIDTH = 5
CONV_PAD = CONV_WIDTH // 2
CHUNK = 64
ROPE_THETA = 500000.0
ROPE_DIM = DIFF_DQK // 4
ROPE_HALF = ROPE_DIM // 2
D_FF = int(math.ceil(8 * D_MODEL / 3 / 256)) * 256

LANES = 128
SUBLANES = 8
VMEM_LIMIT = 56 * 1024 * 1024

ROW_TILE = 512
MIX_FFN_PARTS = 2
GDN_LOCAL_ROWS = 1024
GDN_MATRIX_ROWS = 256
GDN_SCAN_CHUNKS = 4
GDN_SCAN_GROUP = 4
ATTN_Q_BLOCK = 512
ATTN_MAX_CHAINS = 4
ATTN_MAX_STEPS = 32
ATTN_AHEAD = 2
ATTN_S_SLOTS = 4
ATTN_K_BLOCK = 512
ATTN_SUM_ROWS = 16
LOG2E = 1.4426950408889634


def _params(*sem):
    return pltpu.CompilerParams(dimension_semantics=sem, vmem_limit_bytes=VMEM_LIMIT)


def _sigmoid(x):
    return 1.0 / (1.0 + jnp.exp(-x))


def _softplus(x):
    return jnp.maximum(x, 0.0) + jnp.log(1.0 + jnp.exp(-jnp.abs(x)))


def _dot(a, b):
    return jnp.dot(a, b, preferred_element_type=F32)


def _dot_nt(a, b):
    return lax.dot_general(a, b, (((1,), (1,)), ((), ())), preferred_element_type=F32)


def _dot_tn(a, b):
    return lax.dot_general(a, b, (((0,), (0,)), ((), ())), preferred_element_type=F32)


def _layer_norm(y, g, b):
    mu = jnp.mean(y, -1, keepdims=True)
    d = y - mu
    var = jnp.mean(d * d, -1, keepdims=True)
    return d * lax.rsqrt(var + 1e-5) * g + b


def _inproj_kernel(x_ref, wn_ref, wt_ref, cos_ref, sin_ref, kc_ref, ks1_ref, ks2_ref,
                   qkv_ref, z_ref, gt_ref, qt_ref, k_ref, vt_ref):
    xb = x_ref[...].astype(BF16)
    hn = _dot(xb, wn_ref[...])
    qkv_ref[...] = hn[:, :GDN_QKV]
    z_ref[...] = hn[:, GDN_QKV:GDN_QKV + GDN_Z]
    dk0 = GDN_QKV + GDN_Z
    kc, ks1, ks2 = kc_ref[...], ks1_ref[...], ks2_ref[...]
    for h in range(HEADS):
        kh = hn[:, dk0 + h * LANES:dk0 + (h + 1) * LANES]
        up = pltpu.roll(kh, LANES - ROPE_HALF, axis=1)
        dn = pltpu.roll(kh, ROPE_HALF, axis=1)
        k_ref[:, h * LANES:(h + 1) * LANES] = (kh * kc + up * ks1 + dn * ks2).astype(BF16)
    ht = _dot_nt(wt_ref[...], xb)
    vt_ref[...] = ht[DIFF_W:2 * DIFF_W].astype(BF16)
    gt_ref[...] = ht[2 * DIFF_W:2 * DIFF_W + GDN_GATES]
    cos, sin = cos_ref[...], sin_ref[...]
    pieces = []
    for c in range(DIFF_W // DIFF_DQK):
        r = c * DIFF_DQK
        a, b = ht[r:r + ROPE_HALF], ht[r + ROPE_HALF:r + ROPE_DIM]
        pieces += [a * cos - b * sin, b * cos + a * sin, ht[r + ROPE_DIM:r + DIFF_DQK]]
    qt_ref[...] = (jnp.concatenate(pieces, axis=0) * (DIFF_DQK ** -0.5 * LOG2E)).astype(BF16)


def _inproj(x, wn, wt, rope, seq):
    t = x.shape[0]
    tm = ROW_TILE
    nps = seq // tm
    cos_t, sin_t, kc, ks1, ks2 = rope
    row = lambda i: (i, 0)
    col = lambda i: (0, i)
    const = lambda i: (0, 0)
    pos_row = lambda i: (i % nps, 0)
    pos_col = lambda i: (0, i % nps)
    n_cols = wn.shape[1]
    t_rows = wt.shape[0]
    return pl.pallas_call(
        _inproj_kernel,
        grid=(t // tm,),
        in_specs=[
            pl.BlockSpec((tm, D_MODEL), row),
            pl.BlockSpec((D_MODEL, n_cols), const),
            pl.BlockSpec((t_rows, D_MODEL), const),
            pl.BlockSpec((ROPE_HALF, tm), pos_col),
            pl.BlockSpec((ROPE_HALF, tm), pos_col),
            pl.BlockSpec((tm, LANES), pos_row),
            pl.BlockSpec((tm, LANES), pos_row),
            pl.BlockSpec((tm, LANES), pos_row),
        ],
        out_specs=[
            pl.BlockSpec((tm, GDN_QKV), row),
            pl.BlockSpec((tm, GDN_Z), row),
            pl.BlockSpec((GDN_GATES, tm), col),
            pl.BlockSpec((DIFF_W, tm), col),
            pl.BlockSpec((tm, DIFF_W), row),
            pl.BlockSpec((DIFF_W, tm), col),
        ],
        out_shape=[
            jax.ShapeDtypeStruct((t, GDN_QKV), F32),
            jax.ShapeDtypeStruct((t, GDN_Z), F32),
            jax.ShapeDtypeStruct((GDN_GATES, t), F32),
            jax.ShapeDtypeStruct((DIFF_W, t), BF16),
            jax.ShapeDtypeStruct((t, DIFF_W), BF16),
            jax.ShapeDtypeStruct((DIFF_W, t), BF16),
        ],
        compiler_params=_params("parallel"),
        name="inproj",
    )(x, wn, wt, cos_t, sin_t, kc, ks1, ks2)


def _unit_tri_inverses(ms, eye, same16, mid32, same32):
    b16 = lambda a: a.astype(BF16)
    ds = [b16(jnp.where(same16, m, 0.0)) for m in ms]
    c32s = [b16(jnp.where(mid32, m, 0.0)) for m in ms]
    c64s = [b16(jnp.where(same32, 0.0, m)) for m in ms]
    d2s = [b16(_dot(d, d)) for d in ds]
    d4s = [b16(_dot(d2, d2)) for d2 in d2s]
    d8s = [b16(_dot(d4, d4)) for d4 in d4s]
    ts = [eye - d for d in ds]
    for powers in (d2s, d4s, d8s):
        ts = [t + _dot(b16(t), p) for t, p in zip(ts, powers)]
    for corners in (c32s, c64s):
        tbs = [b16(t) for t in ts]
        inner = [b16(_dot(c, tb)) for c, tb in zip(corners, tbs)]
        ts = [t - _dot(tb, i) for t, tb, i in zip(ts, tbs, inner)]
    return ts


def _gdn_local_kernel(qkv_ref, prev_ref, next_ref, gt_ref, cw_ref, pcol_ref,
                      uf_ref, wf_ref, qdf_ref, kdf_ref, af_ref,
                      ub_ref, wb_ref, qdb_ref, kdb_ref, ab_ref, gl_ref, *, blocks_per_seq):
    sb = qkv_ref.shape[0]
    mat = GDN_MATRIX_ROWS
    nsub = sb // mat
    j = pl.program_id(0) % blocks_per_seq
    prev_on = j > 0
    next_on = j < blocks_per_seq - 1
    ext_rows = mat + 2 * SUBLANES
    nch = 2 * HEADS
    u_refs, w_refs, qd_refs, kd_refs, a_refs = ((uf_ref, ub_ref), (wf_ref, wb_ref), (qdf_ref, qdb_ref),
                                                (kdf_ref, kdb_ref), (af_ref, ab_ref))

    def conv_silu(sub):
        r0 = sub * mat
        out = []
        for cg in range(GDN_QKV // LANES):
            sl = slice(cg * LANES, (cg + 1) * LANES)
            before = jnp.where(prev_on, prev_ref[:, sl], 0.0) if sub == 0 else qkv_ref[r0 - SUBLANES:r0, sl]
            after = (jnp.where(next_on, next_ref[:, sl], 0.0) if sub == nsub - 1
                     else qkv_ref[r0 + mat:r0 + mat + SUBLANES, sl])
            ext = jnp.concatenate([before, qkv_ref[r0:r0 + mat, sl], after], axis=0)
            acc = None
            for tap in range(CONV_WIDTH):
                shift = (CONV_PAD - tap) % ext_rows
                rolled = ext if shift == 0 else pltpu.roll(ext, shift, axis=0)
                term = rolled[SUBLANES:SUBLANES + mat] * cw_ref[tap:tap + 1, sl]
                acc = term if acc is None else acc + term
            out.append(acc * _sigmoid(acc))
        return out

    def l2n(t):
        return t * lax.rsqrt(jnp.sum(t * t, -1, keepdims=True) + 1e-6)

    ri = lax.broadcasted_iota(jnp.int32, (mat, mat), 0)
    ci = lax.broadcasted_iota(jnp.int32, (mat, mat), 1)
    same64 = (ri // CHUNK) == (ci // CHUNK)
    same32 = (ri // 32) == (ci // 32)
    same16 = (ri // 16) == (ci // 16)
    mid32 = jnp.logical_and(same32, jnp.logical_not(same16))
    diag = ri == ci
    eye = jnp.where(diag, 1.0, 0.0).astype(F32)
    incl = (jnp.logical_and(same64, ci <= ri), jnp.logical_and(same64, ci >= ri))
    pos = lax.broadcasted_iota(jnp.int32, (nch, mat), 1) % CHUNK
    is_fwd = lax.broadcasted_iota(jnp.int32, (nch, mat), 0) < HEADS

    def wy_factors(sub, conv):
        r0 = sub * mat
        rows_all = slice(r0, r0 + mat)
        g_r = gt_ref[:, rows_all]
        beta_r = _sigmoid(g_r[:nch])
        gval = -jnp.exp(pcol_ref[nch:, 0:1]) * _softplus(g_r[nch:] + pcol_ref[nch:, 1:2])
        pre, suf = gval, gval
        step = 1
        while step < CHUNK:
            pre = pre + jnp.where(pos >= step, pltpu.roll(pre, step, axis=1), 0.0)
            suf = suf + jnp.where(pos < CHUNK - step, pltpu.roll(suf, mat - step, axis=1), 0.0)
            step *= 2
        gtot = pre + suf - gval
        gc_r = jnp.where(is_fwd, pre, suf)
        for c in range(mat // CHUNK):
            gl = jnp.exp(gtot[:, c * CHUNK:c * CHUNK + 1])
            g0 = (r0 // CHUNK + c) * SUBLANES
            gl_ref[g0:g0 + SUBLANES, :] = jnp.broadcast_to(gl, (SUBLANES, LANES))
        stack = jnp.concatenate([beta_r, gc_r, jnp.exp(gc_r), jnp.exp(gtot - gc_r),
                                 jnp.zeros((LANES - 4 * nch, mat), F32)], axis=0)
        cols = stack.T

        qs = [l2n(conv[h]) * (HEAD_DIM ** -0.5) for h in range(HEADS)]
        ks = [l2n(conv[HEADS + h]) for h in range(HEADS)]
        vs = [conv[2 * HEADS + h] for h in range(HEADS)]
        kbs = [k.astype(BF16) for k in ks]
        kks = [_dot_nt(kb, kb) for kb in kbs]
        qks = [_dot_nt(q.astype(BF16), kb) for q, kb in zip(qs, kbs)]

        chains = [(h, d) for h in range(HEADS) for d in range(2)]
        col = lambda group, h, d: cols[:, group * nch + d * HEADS + h:group * nch + d * HEADS + h + 1]
        ms = []
        for h, d in chains:
            r = d * HEADS + h
            decay = jnp.exp(jnp.where(incl[d], col(1, h, d) - gc_r[r:r + 1, :], -jnp.inf))
            ms.append(jnp.where(diag, 0.0, kks[h] * col(0, h, d) * decay))
            attn = (qks[h] * decay).astype(BF16)
            for c in range(mat // CHUNK):
                rows = slice(c * CHUNK, (c + 1) * CHUNK)
                a_refs[d][h, r0 + c * CHUNK:r0 + (c + 1) * CHUNK, :] = attn[rows, rows]
        tinvs = _unit_tri_inverses(ms, eye, same16, mid32, same32)
        rhss = [jnp.concatenate([vs[h] * col(0, h, d), ks[h] * (col(0, h, d) * col(2, h, d))], axis=1).astype(BF16)
                for h, d in chains]
        sols = [_dot(t.astype(BF16), rhs) for t, rhs in zip(tinvs, rhss)]
        for (h, d), sol in zip(chains, sols):
            hs = slice(h * HEAD_DIM, (h + 1) * HEAD_DIM)
            u_refs[d][rows_all, hs] = sol[:, :HEAD_DIM]
            w_refs[d][rows_all, hs] = sol[:, HEAD_DIM:].astype(BF16)
            qd_refs[d][rows_all, hs] = (qs[h] * col(2, h, d)).astype(BF16)
            kd_refs[d][rows_all, hs] = (ks[h] * col(3, h, d)).astype(BF16)

    for sub in range(nsub):
        wy_factors(sub, conv_silu(sub))


def _gdn_local(qkv, gates_t, conv_w8, pcol, seq):
    t = qkv.shape[0]
    sb = GDN_LOCAL_ROWS
    bps = seq // sb
    hb = sb // SUBLANES
    n8 = t // SUBLANES
    row = lambda i: (i, 0)
    const = lambda i: (0, 0)
    wide = jax.ShapeDtypeStruct((t, HEADS * HEAD_DIM), F32)
    wide_bf = jax.ShapeDtypeStruct((t, HEADS * HEAD_DIM), BF16)
    attn = jax.ShapeDtypeStruct((HEADS, t, CHUNK), BF16)
    wide_spec = pl.BlockSpec((sb, HEADS * HEAD_DIM), row)
    attn_spec = pl.BlockSpec((HEADS, sb, CHUNK), lambda i: (0, i, 0))
    dir_shapes = [wide, wide_bf, wide_bf, wide_bf, attn]
    dir_specs = [wide_spec, wide_spec, wide_spec, wide_spec, attn_spec]
    return pl.pallas_call(
        functools.partial(_gdn_local_kernel, blocks_per_seq=bps),
        grid=(t // sb,),
        in_specs=[
            pl.BlockSpec((sb, GDN_QKV), row),
            pl.BlockSpec((SUBLANES, GDN_QKV), lambda i: (jnp.maximum(i * hb - 1, 0), 0)),
            pl.BlockSpec((SUBLANES, GDN_QKV), lambda i: (jnp.minimum((i + 1) * hb, n8 - 1), 0)),
            pl.BlockSpec((GDN_GATES, sb), lambda i: (0, i)),
            pl.BlockSpec((SUBLANES, GDN_QKV), const),
            pl.BlockSpec((GDN_GATES, LANES), const),
        ],
        out_specs=dir_specs + dir_specs + [pl.BlockSpec((sb // CHUNK * SUBLANES, LANES), row)],
        out_shape=dir_shapes + dir_shapes + [jax.ShapeDtypeStruct((t // CHUNK * SUBLANES, LANES), F32)],
        compiler_params=_params("parallel"),
        name="gdn_local",
    )(qkv, qkv, qkv, gates_t, conv_w8, pcol)


def _gdn_scan_kernel(uf_ref, wf_ref, qdf_ref, kdf_ref, af_ref, glf_ref,
                     ub_ref, wb_ref, qdb_ref, kdb_ref, ab_ref, glb_ref,
                     of_ref, ob_ref, state_ref, *, chunks, group):
    @pl.when(pl.program_id(1) == 0)
    def _():
        state_ref[...] = jnp.zeros_like(state_ref)

    dirs = ((uf_ref, wf_ref, qdf_ref, kdf_ref, af_ref, glf_ref, of_ref),
            (ub_ref, wb_ref, qdb_ref, kdb_ref, ab_ref, glb_ref, ob_ref))
    chains = [(g, d, h) for g in range(group) for d in range(2) for h in range(HEADS)]

    def body(c, carry):
        chunk = (c, chunks - 1 - c)
        rows = tuple(pl.ds(pl.multiple_of(ch * CHUNK, CHUNK), CHUNK) for ch in chunk)
        hs = lambda h: slice(h * HEAD_DIM, (h + 1) * HEAD_DIM)
        slot = lambda g, d, h: (g * 2 + d) * HEADS + h
        ws_qs = []
        for g, d, h in chains:
            w_ref, qd_ref = dirs[d][1], dirs[d][2]
            lhs = jnp.concatenate([w_ref[g, rows[d], hs(h)], qd_ref[g, rows[d], hs(h)]], axis=0)
            ws_qs.append(_dot(lhs, state_ref[slot(g, d, h)].astype(BF16)))
        vbs = [(dirs[d][0][g, rows[d], hs(h)] - wq[:CHUNK]).astype(BF16) for (g, d, h), wq in zip(chains, ws_qs)]
        for (g, d, h), wq, vb in zip(chains, ws_qs, vbs):
            kd_ref, a_ref, gl_ref, o_ref = dirs[d][3], dirs[d][4], dirs[d][5], dirs[d][6]
            o_ref[g, rows[d], hs(h)] = wq[CHUNK:] + _dot(a_ref[h, g, rows[d], :], vb)
            gl = gl_ref[g, pl.ds(chunk[d] * SUBLANES + d * HEADS + h, 1), :]
            state_ref[slot(g, d, h)] = state_ref[slot(g, d, h)] * gl + _dot_tn(kd_ref[g, rows[d], hs(h)], vb)
        return carry

    lax.fori_loop(0, chunks, body, 0)


def _gdn_scan(local_out, batch, seq):
    cb = GDN_SCAN_CHUNKS
    grp = GDN_SCAN_GROUP
    rb = cb * CHUNK
    nb = seq // rb
    wide = lambda a: a.reshape(batch, seq, HEADS * HEAD_DIM)
    uf, wf, qdf, kdf, af, ub, wb, qdb, kdb, ab, gl = local_out
    uf, wf, qdf, kdf, ub, wb, qdb, kdb = (wide(a) for a in (uf, wf, qdf, kdf, ub, wb, qdb, kdb))
    af, ab = (a.reshape(HEADS, batch, seq, CHUNK) for a in (af, ab))
    gl = gl.reshape(batch, seq // CHUNK * SUBLANES, LANES)
    fwd = lambda b, j: (b, j, 0)
    bwd = lambda b, j: (b, nb - 1 - j, 0)
    fwd4 = lambda b, j: (0, b, j, 0)
    bwd4 = lambda b, j: (0, b, nb - 1 - j, 0)

    def dir_specs(m3, m4):
        blk = pl.BlockSpec((grp, rb, HEADS * HEAD_DIM), m3)
        return [blk, blk, blk, blk, pl.BlockSpec((HEADS, grp, rb, CHUNK), m4),
                pl.BlockSpec((grp, cb * SUBLANES, LANES), m3)]

    out = jax.ShapeDtypeStruct((batch, seq, HEADS * HEAD_DIM), F32)
    o_f, o_b = pl.pallas_call(
        functools.partial(_gdn_scan_kernel, chunks=cb, group=grp),
        grid=(batch // grp, nb),
        in_specs=dir_specs(fwd, fwd4) + dir_specs(bwd, bwd4),
        out_specs=[pl.BlockSpec((grp, rb, HEADS * HEAD_DIM), fwd), pl.BlockSpec((grp, rb, HEADS * HEAD_DIM), bwd)],
        out_shape=[out, out],
        scratch_shapes=[pltpu.VMEM((grp * 2 * HEADS, HEAD_DIM, HEAD_DIM), F32)],
        compiler_params=_params("parallel", "arbitrary"),
        name="gdn_scan",
    )(uf, wf, qdf, kdf, af, gl, ub, wb, qdb, kdb, ab, gl)
    return o_f.reshape(batch * seq, HEADS * HEAD_DIM), o_b.reshape(batch * seq, HEADS * HEAD_DIM)


def _attn_kernel(qt_ref, k_ref, vt_ref, lq_ref, g_ref, o_ref, s_ref, p_ref, acc_ref, *, seq, kblk, qb, lam_init):
    nblk = seq // kblk
    nchain = qt_ref.shape[1] // qb
    row = lax.broadcasted_iota(jnp.int32, (2 * DIFF_DQK, 2 * qb), 0)
    colq = lax.broadcasted_iota(jnp.int32, (2 * DIFF_DQK, 2 * qb), 1)
    keep = (row < DIFF_DQK) == (colq < qb)
    rhs = []
    for c in range(nchain):
        qt = qt_ref[:, c * qb:(c + 1) * qb]
        rhs.append(jnp.where(keep, jnp.concatenate([qt, qt], axis=1), jnp.zeros((), BF16)))
    ones = jnp.ones((ATTN_SUM_ROWS, kblk), BF16)
    lq = lq_ref[...]
    lam = (jnp.exp(jnp.sum(lq[0:1] * lq[1:2], axis=-1, keepdims=True))
           - jnp.exp(jnp.sum(lq[2:3] * lq[3:4], axis=-1, keepdims=True)) + lam_init)

    def key_rows(i):
        return pl.ds(i * kblk, kblk)

    groups = [(g, slice(g * LANES, (g + 1) * LANES)) for g in range(2 * qb // LANES)]

    def score_block(c, i, slot):
        sc = _dot(k_ref[key_rows(i), :], rhs[c])
        for g, ls in groups:
            s_ref[c, slot, g, :kblk] = sc[:, ls]

    def weighted_values(c, i, slot):
        lhs = jnp.concatenate([vt_ref[:, key_rows(i)], ones], axis=0)
        return _dot(lhs, jnp.concatenate([p_ref[c, slot, g, :kblk] for g, _ in groups], axis=1))

    def start(c):
        for i in range(ATTN_AHEAD):
            score_block(c, i, i)
        return [jnp.full((1, LANES), -jnp.inf, F32)] * len(groups)

    def step(c, i, m):
        if i > 0:
            pv = weighted_values(c, i - 1, (i - 1) % 2)
        if i + ATTN_AHEAD < nblk:
            score_block(c, i + ATTN_AHEAD, (i + ATTN_AHEAD) % ATTN_S_SLOTS)
        m_out = []
        for g, ls in groups:
            s = s_ref[c, i % ATTN_S_SLOTS, g, :kblk]
            m_new = jnp.maximum(m[g], jnp.max(s, axis=0, keepdims=True))
            p_ref[c, i % 2, g, :kblk] = jnp.exp2(s - m_new).astype(BF16)
            if i > 0:
                alpha = jnp.exp2(m[g] - m_new)
                acc_ref[c, g] = (pv[:, ls] if i == 1 else acc_ref[c, g] + pv[:, ls]) * alpha
            m_out.append(m_new)
        return m_out

    def finish(c):
        pv = weighted_values(c, nblk - 1, (nblk - 1) % 2)
        acc = jnp.concatenate([acc_ref[c, g] + pv[:, ls] for g, ls in groups], axis=1)
        on = acc[:HEAD_DIM] / acc[HEAD_DIM:HEAD_DIM + 1]
        o = (on[:, :qb] - lam * on[:, qb:]).T
        o = o * lax.rsqrt(jnp.mean(o * o, -1, keepdims=True) + 1e-6) * g_ref[...]
        o_ref[c * qb:(c + 1) * qb, :] = (o * (1.0 - lam_init)).astype(o_ref.dtype)

    state = start(0)
    for c in range(nchain):
        for i in range(nblk):
            if c + 1 < nchain and i == nblk - ATTN_AHEAD:
                upcoming = start(c + 1)
            state = step(c, i, state)
        finish(c)
        if c + 1 < nchain:
            state = upcoming


def _attention(qt, k, vt, lam_qk, norm_g, batch, seq, lam_init):
    t = k.shape[0]
    qb = ATTN_Q_BLOCK
    kblk = ATTN_K_BLOCK
    nc = min(ATTN_MAX_CHAINS, seq // qb)
    while (seq // qb) % nc or nc * (seq // kblk) > ATTN_MAX_STEPS:
        nc -= 1
    nq = seq // (nc * qb)
    ng = 2 * qb // LANES
    return pl.pallas_call(
        functools.partial(_attn_kernel, seq=seq, kblk=kblk, qb=qb, lam_init=lam_init),
        grid=(batch, HEADS, nq),
        in_specs=[
            pl.BlockSpec((2 * DIFF_DQK, nc * qb), lambda b, h, i: (h, b * nq + i)),
            pl.BlockSpec((seq, 2 * DIFF_DQK), lambda b, h, i: (b, h)),
            pl.BlockSpec((HEAD_DIM, seq), lambda b, h, i: (h, b)),
            pl.BlockSpec((4, DIFF_DQK), lambda b, h, i: (0, 0)),
            pl.BlockSpec((1, HEAD_DIM), lambda b, h, i: (0, 0)),
        ],
        out_specs=pl.BlockSpec((nc * qb, HEAD_DIM), lambda b, h, i: (b * nq + i, h)),
        out_shape=jax.ShapeDtypeStruct((t, HEADS * HEAD_DIM), BF16),
        scratch_shapes=[pltpu.VMEM((nc, ATTN_S_SLOTS, ng, kblk, LANES), F32),
                        pltpu.VMEM((nc, 2, ng, kblk, LANES), BF16),
                        pltpu.VMEM((nc, ng, HEAD_DIM + ATTN_SUM_ROWS, LANES), F32)],
        compiler_params=_params("parallel", "parallel", "arbitrary"),
        name="diff_attn",
    )(qt, k, vt, lam_qk, norm_g)


def _mix_ffn_kernel(of_ref, ob_ref, z_ref, oa_ref, x_ref, wo_ref, gg_ref, l1g_ref, l1b_ref,
                    wg_ref, wu_ref, wd_ref, l2g_ref, l2b_ref, y_ref, *, alpha, parts):
    rows = [slice(i * (x_ref.shape[0] // parts), (i + 1) * (x_ref.shape[0] // parts)) for i in range(parts)]

    def mixed(rs):
        cols = []
        for h in range(HEADS):
            hs = slice(h * HEAD_DIM, (h + 1) * HEAD_DIM)
            o = of_ref[rs, hs] + ob_ref[rs, hs]
            o = o * lax.rsqrt(jnp.mean(o * o, -1, keepdims=True) + 1e-6) * gg_ref[...]
            zz = z_ref[rs, hs]
            cols.append((o * (zz * _sigmoid(zz))).astype(BF16))
        return jnp.concatenate(cols + [oa_ref[rs, :]], axis=1)

    mixes = [mixed(rs) for rs in rows]
    x1s = [_layer_norm(alpha * x_ref[rs, :] + _dot(mx, wo_ref[...]), l1g_ref[...], l1b_ref[...])
           for rs, mx in zip(rows, mixes)]
    xbs = [x1.astype(BF16) for x1 in x1s]
    gates = [_dot(xb, wg_ref[...]) for xb in xbs]
    ups = [_dot(xb, wu_ref[...]) for xb in xbs]
    acts = [(gate * _sigmoid(gate) * up).astype(BF16) for gate, up in zip(gates, ups)]
    for rs, x1, act in zip(rows, x1s, acts):
        y_ref[rs, :] = _layer_norm(alpha * x1 + _dot(act, wd_ref[...]), l2g_ref[...], l2b_ref[...])


def _mix_ffn(o_f, o_b, z, o_attn, x, p, alpha):
    t = x.shape[0]
    tm = ROW_TILE
    row = lambda i: (i, 0)
    const = lambda i: (0, 0)
    once = pl.Buffered(1)
    half = pl.BlockSpec((tm, HEADS * HEAD_DIM), row)
    vec = pl.BlockSpec((1, D_MODEL), const)
    return pl.pallas_call(
        functools.partial(_mix_ffn_kernel, alpha=alpha, parts=MIX_FFN_PARTS),
        grid=(t // tm,),
        in_specs=[half, half, half, half,
                  pl.BlockSpec((tm, D_MODEL), row),
                  pl.BlockSpec((D_MODEL, D_MODEL), const, pipeline_mode=once),
                  pl.BlockSpec((1, HEAD_DIM), const), vec, vec,
                  pl.BlockSpec((D_MODEL, D_FF), const, pipeline_mode=once),
                  pl.BlockSpec((D_MODEL, D_FF), const, pipeline_mode=once),
                  pl.BlockSpec((D_FF, D_MODEL), const, pipeline_mode=once),
                  vec, vec],
        out_specs=pl.BlockSpec((tm, D_MODEL), row),
        out_shape=jax.ShapeDtypeStruct((t, D_MODEL), F32),
        compiler_params=_params("parallel"),
        name="mix_ffn",
    )(o_f, o_b, z, o_attn, x, p["w_out"], p["gdn_g"], p["ln1_g"], p["ln1_b"],
      p["wg"], p["wu"], p["wd"], p["ln2_g"], p["ln2_b"])


def _rope_tables(seq):
    inv = 1.0 / (ROPE_THETA ** (jnp.arange(0, ROPE_DIM, 2, dtype=F32) / ROPE_DIM))
    ang = jnp.arange(seq, dtype=F32)[:, None] * inv[None, :]
    cos, sin = jnp.cos(ang), jnp.sin(ang)
    zero8 = jnp.zeros_like(sin)
    rest = DIFF_DQK - ROPE_DIM
    c64 = jnp.concatenate([cos, cos, jnp.ones((seq, rest), F32)], axis=1)
    s1 = jnp.concatenate([-sin, zero8, jnp.zeros((seq, rest), F32)], axis=1)
    s2 = jnp.concatenate([zero8, sin, jnp.zeros((seq, rest), F32)], axis=1)
    two = lambda a: jnp.concatenate([a, a], axis=1)
    return cos.T, sin.T, two(c64), two(s1), two(s2)


def _prep_layer(w_in, conv_w, a_log, dt_bias, gdn_norm_g, lam_qk, diff_norm_g, w_out, ln1_g, ln1_b,
                w_gate_up, w_down, ln2_g, ln2_b):
    c0 = GDN_QKV
    c1 = c0 + GDN_Z
    c2 = c1 + GDN_GATES
    c3 = c2 + DIFF_W
    c4 = c3 + DIFF_W
    w_g = w_in[:, c1:c2]
    wn = jnp.concatenate([w_in[:, :c1], w_in[:, c3:c4]], axis=1).astype(BF16)
    wt = jnp.concatenate([w_in[:, c2:c3], w_in[:, c4:], w_g], axis=1).T.astype(BF16)
    conv_w8 = jnp.pad(conv_w, ((0, SUBLANES - CONV_WIDTH), (0, 0)))
    gate_par = jnp.stack([a_log.reshape(-1), dt_bias.reshape(-1)])
    pcol = jnp.pad(gate_par.T, ((2 * HEADS, 0), (0, LANES - 2)))
    return dict(
        wn=wn, wt=wt, conv_w8=conv_w8, pcol=pcol,
        gdn_g=gdn_norm_g.reshape(1, HEAD_DIM), lam_qk=lam_qk, diff_g=diff_norm_g.reshape(1, HEAD_DIM),
        w_out=w_out.astype(BF16), ln1_g=ln1_g.reshape(1, D_MODEL), ln1_b=ln1_b.reshape(1, D_MODEL),
        wg=w_gate_up[:, :D_FF].astype(BF16), wu=w_gate_up[:, D_FF:].astype(BF16), wd=w_down.astype(BF16),
        ln2_g=ln2_g.reshape(1, D_MODEL), ln2_b=ln2_b.reshape(1, D_MODEL))


def _trunk(x, layers, alpha):
    batch, seq, _ = x.shape
    rope = _rope_tables(seq)
    xf = x.reshape(batch * seq, D_MODEL)
    for l, p in enumerate(layers):
        lam_init = 0.8 - 0.6 * math.exp(-0.3 * l)
        qkv, z, gates_t, qt, k, vt = _inproj(xf, p["wn"], p["wt"], rope, seq)
        local = _gdn_local(qkv, gates_t, p["conv_w8"], p["pcol"], seq)
        o_f, o_b = _gdn_scan(local, batch, seq)
        o_attn = _attention(qt, k, vt, p["lam_qk"], p["diff_g"], batch, seq, lam_init)
        xf = _mix_ffn(o_f, o_b, z, o_attn, xf, p, alpha)
    return xf.reshape(batch, seq, D_MODEL)


def kernel(x_prompt, x_sample, w_in, conv_w, a_log, dt_bias, gdn_norm_g, lam_qk, diff_norm_g, w_out,
           ln1_g, ln1_b, w_gate_up, w_down, ln2_g, ln2_b):
    depth = w_in.shape[0]
    alpha = (2 * depth) ** 0.25
    layers = [_prep_layer(w_in[l], conv_w[l], a_log[l], dt_bias[l], gdn_norm_g[l], lam_qk[l], diff_norm_g[l],
                          w_out[l], ln1_g[l], ln1_b[l], w_gate_up[l], w_down[l], ln2_g[l], ln2_b[l])
              for l in range(depth)]
    return (_trunk(x_prompt, layers, alpha), _trunk(x_sample, layers, alpha))
```
